```python
import jax, jax.numpy as jnp
from jax import lax
import numpy as np

D_MODEL = 1024
BATCH = 8
SEQ = 2048
DEPTH = 1
DEC_BATCH = 16
DEC_SEQ = 64
PAST_LEN = 4096

CHUNK = 64
N_META = 16
POOL_W = D_MODEL // 2
POOL_WINDOWS = (2, 4, 8, 16)
N_POOL_GROUPS = len(POOL_WINDOWS)
POOL_GC = POOL_W // N_POOL_GROUPS
POOL_STATE = max(POOL_WINDOWS) - 1
SB_HEADS = 8
SB_HEAD_DIM = 64
ATTN_W = SB_HEADS * SB_HEAD_DIM
QBLOCK = 128
D_FF = 4 * D_MODEL
IN_W = POOL_W + 3 * ATTN_W + 2 * D_MODEL
LN_EPS = 1e-5
ALPHA = (2.0 * DEPTH) ** 0.25
BETA_INIT = (8.0 * DEPTH) ** -0.25
SB_SCALE = SB_HEAD_DIM ** -0.5

kernel_name = "hybrid_pool_stickbreak_streaming_step"


def _layer_norm(x, g, b):
    xf = x.astype(jnp.float32)
    mu = jnp.mean(xf, axis=-1, keepdims=True)
    var = jnp.mean(jnp.square(xf - mu), axis=-1, keepdims=True)
    return ((xf - mu) * lax.rsqrt(var + LN_EPS) * g + b).astype(x.dtype)


def _multiscale_pool(u, u_past, n_past):
    B, T, C = u.shape
    full = jnp.concatenate([u_past, u], axis=1)
    csum = jnp.concatenate([jnp.zeros((B, 1, C), jnp.float32),
                            jnp.cumsum(full.astype(jnp.float32), axis=1)], axis=1)
    hi = csum[:, POOL_STATE + 1:]
    pos = jnp.arange(T)
    outs = []
    for g, w in enumerate(POOL_WINDOWS):
        sl = slice(g * POOL_GC, (g + 1) * POOL_GC)
        lo = csum[:, POOL_STATE + 1 - w: POOL_STATE + 1 - w + T, sl]
        cnt = jnp.minimum(w, n_past + pos + 1).astype(jnp.float32)[None, :, None]
        outs.append((hi[..., sl] - lo) / cnt)
    mean = jnp.concatenate(outs, axis=-1)
    pooled = (mean - u.astype(jnp.float32)).astype(u.dtype)
    return pooled, full[:, -POOL_STATE:]


def _sb_block(q_blk, q_idx, keys, vals):
    z = jnp.einsum('bqhd,bkhd->bhqk', q_blk, keys).astype(jnp.float32) * SB_SCALE
    visible = jnp.arange(keys.shape[1])[None, :] < q_idx[:, None]
    log_keep = jnp.where(visible, jax.nn.log_sigmoid(-z), 0.0)
    tail = lax.cumsum(log_keep, axis=3, reverse=True) - log_keep
    weight = jnp.where(visible, jnp.exp(jax.nn.log_sigmoid(z) + tail), 0.0)
    return jnp.einsum('bhqk,bkhd->bqhd', weight.astype(vals.dtype), vals)


def _stick_breaking_attention(q, keys, vals, n_hist):
    B, T, H, Dh = q.shape
    nblk = -(-T // QBLOCK)
    t_pad = nblk * QBLOCK
    q_pad = jnp.pad(q, ((0, 0), (0, t_pad - T), (0, 0), (0, 0)))
    q_blocks = q_pad.reshape(B, nblk, QBLOCK, H, Dh).transpose(1, 0, 2, 3, 4)
    idx_blocks = (n_hist + jnp.arange(t_pad)).reshape(nblk, QBLOCK)
    out = lax.map(lambda a: _sb_block(a[0], a[1], keys, vals), (q_blocks, idx_blocks))
    return out.transpose(1, 0, 2, 3, 4).reshape(B, t_pad, H, Dh)[:, :T]


def _layer(h, k_hist, v_hist, u_past, n_past, w_in, b_in, w_pool_grp, pool_scale, w_br_pool, w_br_attn,
           w_out, b_out, ln1_g, ln1_b, w_mlp1, b_mlp1, w_mlp2, b_mlp2, ln2_g, ln2_b):
    B, T, _ = h.shape
    z = h @ w_in + b_in
    u, q, k, v, ga, gb = jnp.split(
        z, [POOL_W, POOL_W + ATTN_W, POOL_W + 2 * ATTN_W, POOL_W + 3 * ATTN_W,
            POOL_W + 3 * ATTN_W + D_MODEL], axis=-1)
    pooled, u_state = _multiscale_pool(u, u_past, n_past)
    pooled = pooled.reshape(B, T, N_POOL_GROUPS, POOL_GC)
    pool_out = jnp.einsum('btgc,gcd->btgd', pooled, w_pool_grp).reshape(B, T, POOL_W) * pool_scale
    q = q.reshape(B, T, SB_HEADS, SB_HEAD_DIM)
    k = k.reshape(B, T, SB_HEADS, SB_HEAD_DIM)
    v = v.reshape(B, T, SB_HEADS, SB_HEAD_DIM)
    keys = jnp.concatenate([k_hist, k], axis=1)
    vals = jnp.concatenate([v_hist, v], axis=1)
    attn = _stick_breaking_attention(q, keys, vals, k_hist.shape[1]).reshape(B, T, ATTN_W)
    mix = jax.nn.sigmoid(ga) * (pool_out @ w_br_pool) + jax.nn.sigmoid(gb) * (attn @ w_br_attn)
    h = _layer_norm(ALPHA * h + (mix @ w_out + b_out), ln1_g, ln1_b)
    f = jnp.square(jax.nn.relu(h @ w_mlp1 + b_mlp1)) @ w_mlp2 + b_mlp2
    h = _layer_norm(ALPHA * h + f, ln2_g, ln2_b)
    return h, k, v, u_state


def setup_inputs(seed: int = 0) -> dict:
    key = jax.random.key(seed)
    ks = jax.random.split(key, 24)

    def nrm(k, shape, scale):
        return jax.random.normal(k, shape, jnp.float32) * scale

    return {
        "x_prompt": nrm(ks[0], (BATCH, SEQ, D_MODEL), 1.0),
        "x_sample": nrm(ks[1], (DEC_BATCH, DEC_SEQ, D_MODEL), 1.0),
        "cache_k": nrm(ks[2], (DEPTH, DEC_BATCH, PAST_LEN, SB_HEADS, SB_HEAD_DIM), 1.0),
        "cache_v": nrm(ks[3], (DEPTH, DEC_BATCH, PAST_LEN, SB_HEADS, SB_HEAD_DIM), 1.0),
        "state_pool": nrm(ks[4], (DEPTH, DEC_BATCH, POOL_STATE, POOL_W), 1.0),
        "meta": nrm(ks[5], (N_META, D_MODEL), 1.0),
        "ln_in_g": 1.0 + nrm(ks[6], (D_MODEL,), 0.05),
        "ln_in_b": nrm(ks[7], (D_MODEL,), 0.02),
        "w_in": nrm(ks[8], (DEPTH, D_MODEL, IN_W), D_MODEL ** -0.5),
        "b_in": nrm(ks[9], (DEPTH, IN_W), 0.02),
        "w_pool_grp": nrm(ks[10], (DEPTH, N_POOL_GROUPS, POOL_GC, POOL_GC), POOL_GC ** -0.5),
        "pool_scale": 1.0 + nrm(ks[11], (DEPTH, POOL_W), 0.1),
        "w_br_pool": nrm(ks[12], (DEPTH, POOL_W, D_MODEL), POOL_W ** -0.5 * BETA_INIT),
        "w_br_attn": nrm(ks[13], (DEPTH, ATTN_W, D_MODEL), ATTN_W ** -0.5 * BETA_INIT),
        "w_out": nrm(ks[14], (DEPTH, D_MODEL, D_MODEL), D_MODEL ** -0.5 * BETA_INIT),
        "b_out": nrm(ks[15], (DEPTH, D_MODEL), 0.02),
        "ln1_g": 1.0 + nrm(ks[16], (DEPTH, D_MODEL), 0.05),
        "ln1_b": nrm(ks[17], (DEPTH, D_MODEL), 0.02),
        "w_mlp1": nrm(ks[18], (DEPTH, D_MODEL, D_FF), D_MODEL ** -0.5 * BETA_INIT),
        "b_mlp1": nrm(ks[19], (DEPTH, D_FF), 0.02),
        "w_mlp2": nrm(ks[20], (DEPTH, D_FF, D_MODEL), D_FF ** -0.5 * BETA_INIT),
        "b_mlp2": nrm(ks[21], (DEPTH, D_MODEL), 0.02),
        "ln2_g": 1.0 + nrm(ks[22], (DEPTH, D_MODEL), 0.05),
        "ln2_b": nrm(ks[23], (DEPTH, D_MODEL), 0.02),
    }


def reference(x_prompt, x_sample, cache_k, cache_v, state_pool, meta, ln_in_g, ln_in_b, w_in, b_in,
              w_pool_grp, pool_scale, w_br_pool, w_br_attn, w_out, b_out, ln1_g, ln1_b,
              w_mlp1, b_mlp1, w_mlp2, b_mlp2, ln2_g, ln2_b):
    B = x_prompt.shape[0]
    meta_rows = jnp.broadcast_to(meta.astype(x_prompt.dtype)[None], (B, N_META, D_MODEL))
    hp = _layer_norm(jnp.concatenate([meta_rows, x_prompt], axis=1), ln_in_g, ln_in_b)
    hs = _layer_norm(x_sample, ln_in_g, ln_in_b)
    kp_l, vp_l, up_l, ks_l, vs_l, us_l = [], [], [], [], [], []
    for l in range(DEPTH):
        wl = (w_in[l], b_in[l], w_pool_grp[l], pool_scale[l], w_br_pool[l], w_br_attn[l], w_out[l], b_out[l],
              ln1_g[l], ln1_b[l], w_mlp1[l], b_mlp1[l], w_mlp2[l], b_mlp2[l], ln2_g[l], ln2_b[l])
        zero_hist = jnp.zeros((B, 0, SB_HEADS, SB_HEAD_DIM), hp.dtype)
        zero_pool = jnp.zeros((B, POOL_STATE, POOL_W), hp.dtype)
        hp, kp, vp, up = _layer(hp, zero_hist, zero_hist, zero_pool, 0, *wl)
        hs, ks_, vs_, us = _layer(hs, cache_k[l], cache_v[l], state_pool[l], POOL_STATE, *wl)
        kp_l.append(kp); vp_l.append(vp); up_l.append(up)
        ks_l.append(ks_); vs_l.append(vs_); us_l.append(us)
    y_prompt = hp[:, N_META:]
    return (y_prompt, hs, jnp.stack(kp_l), jnp.stack(vp_l), jnp.stack(up_l),
            jnp.stack(ks_l), jnp.stack(vs_l), jnp.stack(us_l))
```

```python
import functools

import jax
import jax.numpy as jnp
from jax import lax
from jax.experimental import pallas as pl
from jax.experimental.pallas import tpu as pltpu

D_MODEL = 1024
N_META = 16
POOL_W = D_MODEL // 2
POOL_WINDOWS = (2, 4, 8, 16)
POOL_GC = POOL_W // len(POOL_WINDOWS)
POOL_STATE = max(POOL_WINDOWS) - 1
POOL_HALO = POOL_STATE + 1
SB_HEADS = 8
SB_HEAD_DIM = 64
ATTN_W = SB_HEADS * SB_HEAD_DIM
D_FF = 4 * D_MODEL
UQKV_W = POOL_W + 3 * ATTN_W
LN_EPS = 1e-5
DEPTH = 1
ALPHA = (2.0 * DEPTH) ** 0.25
SB_SCALE = SB_HEAD_DIM ** -0.5

LANES = 128
KEY_BLOCK = LANES
HEAD_PAIRS = ATTN_W // LANES
VMEM_LIMIT_BYTES = 56 * 1024 * 1024

F32 = jnp.float32
BF16 = jnp.bfloat16


def _layer_norm(x, g, b):
    mu = jnp.mean(x, axis=-1, keepdims=True)
    xc = x - mu
    var = jnp.mean(xc * xc, axis=-1, keepdims=True)
    return xc * lax.rsqrt(var + LN_EPS) * g + b


def _dot(a, b):
    return jnp.dot(a, b, preferred_element_type=F32)


def _whole():
    return pl.BlockSpec(memory_space=pltpu.VMEM)


def _inproj_kernel(x_ref, g_ref, b_ref, w_ref, bias_ref,
                   u_ref, q_ref, kf_ref, vf_ref, kb_ref, vb_ref):
    h = _layer_norm(x_ref[...], g_ref[...], b_ref[...])
    z = _dot(h.astype(BF16), w_ref[...]) + bias_ref[...]
    u_ref[...] = z[:, :POOL_W]
    q_ref[...] = (z[:, POOL_W:POOL_W + ATTN_W] * SB_SCALE).astype(BF16)
    k = z[:, POOL_W + ATTN_W:POOL_W + 2 * ATTN_W]
    v = z[:, POOL_W + 2 * ATTN_W:]
    kf_ref[...] = k
    vf_ref[...] = v
    kb_ref[...] = k.astype(BF16)
    vb_ref[...] = v.astype(BF16)


def _inproj(x, ln_g, ln_b, w_uqkv, b_uqkv, tm):
    rows = x.shape[0]
    assert rows % tm == 0
    row_spec = lambda w: pl.BlockSpec((tm, w), lambda i: (i, 0))
    out_shape = (
        jax.ShapeDtypeStruct((rows, POOL_W), F32),
        jax.ShapeDtypeStruct((rows, ATTN_W), BF16),
        jax.ShapeDtypeStruct((rows, ATTN_W), F32),
        jax.ShapeDtypeStruct((rows, ATTN_W), F32),
        jax.ShapeDtypeStruct((rows, ATTN_W), BF16),
        jax.ShapeDtypeStruct((rows, ATTN_W), BF16),
    )
    return pl.pallas_call(
        _inproj_kernel,
        grid=(rows // tm,),
        in_specs=[row_spec(D_MODEL), _whole(), _whole(), _whole(), _whole()],
        out_specs=tuple(row_spec(s.shape[1]) for s in out_shape),
        out_shape=out_shape,
        name="inproj",
        compiler_params=pltpu.CompilerParams(
            dimension_semantics=("arbitrary",), vmem_limit_bytes=VMEM_LIMIT_BYTES),
    )(x, ln_g, ln_b, w_uqkv, b_uqkv)


def _pool_kernel(halo_ref, u_ref, o_ref):
    t = u_ref.shape[1]
    full = jnp.concatenate([halo_ref[0], u_ref[0]], axis=0)
    outs = []
    for g, w in enumerate(POOL_WINDOWS):
        x = full[:, g * POOL_GC:(g + 1) * POOL_GC]
        s = x
        span = 1
        while span < w:
            s = s + pltpu.roll(s, span, axis=0)
            span *= 2
        outs.append(s[POOL_HALO:] * (1.0 / w) - x[POOL_HALO:])
    o_ref[0] = jnp.concatenate(outs, axis=-1).astype(o_ref.dtype)


def _pool(u, halo, shared_halo):
    b, t, _ = u.shape
    halo_map = (lambda i: (0, 0, 0)) if shared_halo else (lambda i: (i, 0, 0))
    return pl.pallas_call(
        _pool_kernel,
        grid=(b,),
        in_specs=[pl.BlockSpec((1, POOL_HALO, POOL_W), halo_map),
                  pl.BlockSpec((1, t, POOL_W), lambda i: (i, 0, 0))],
        out_specs=pl.BlockSpec((1, t, POOL_W), lambda i: (i, 0, 0)),
        out_shape=jax.ShapeDtypeStruct((b, t, POOL_W), BF16),
        name="pool",
        compiler_params=pltpu.CompilerParams(
            dimension_semantics=("arbitrary",), vmem_limit_bytes=VMEM_LIMIT_BYTES),
    )(halo, u)


def _attn_kernel(q_ref, kn_ref, vn_ref, kh_ref, vh_ref, su_ref, o_ref, *,
                 tq, n_hist_blocks, hist_valid):
    i = pl.program_id(2)
    lane = lax.broadcasted_iota(jnp.int32, (tq, LANES), 1)
    q = q_ref[0]
    zero = jnp.zeros_like(q)
    qs = jnp.concatenate([jnp.where(lane < SB_HEAD_DIM, q, zero),
                          jnp.where(lane < SB_HEAD_DIM, zero, q)], axis=0)
    su = su_ref[...]

    row = lax.broadcasted_iota(jnp.int32, (2 * tq, KEY_BLOCK), 0)
    col = lax.broadcasted_iota(jnp.int32, (2 * tq, KEY_BLOCK), 1)
    q_pos = jnp.where(row >= tq, row - tq, row) + i * tq

    def step(kb, vb, carry, vis):
        c, acc = carry
        s = lax.dot_general(qs, kb, (((1,), (1,)), ((), ())), preferred_element_type=F32)
        lk = -(jnp.maximum(s, 0.0) + jnp.log(1.0 + jnp.exp(-jnp.abs(s))))
        lb = s + lk
        if vis is not None:
            lk = jnp.where(vis, lk, 0.0)
        hi = lk.astype(BF16)
        lo = (lk - hi.astype(F32)).astype(BF16)
        tl = _dot(jnp.concatenate([hi, lo], axis=1), su)
        w = jnp.exp(lb + tl[:, :KEY_BLOCK] + c)
        if vis is not None:
            w = jnp.where(vis, w, 0.0)
        acc = acc + _dot(w.astype(BF16), vb)
        return c + tl[:, KEY_BLOCK:], acc

    carry = (jnp.zeros((2 * tq, LANES), F32), jnp.zeros((2 * tq, LANES), F32))

    n_diag = max(tq // KEY_BLOCK, 1)
    first_diag = i * (tq // KEY_BLOCK) if tq >= KEY_BLOCK else (i * tq) // KEY_BLOCK
    for d in reversed(range(n_diag)):
        start = pl.multiple_of((first_diag + d) * KEY_BLOCK, KEY_BLOCK)
        vis = (first_diag + d) * KEY_BLOCK + col < q_pos
        carry = step(kn_ref[0, pl.ds(start, KEY_BLOCK), :], vn_ref[0, pl.ds(start, KEY_BLOCK), :],
                     carry, vis)

    def new_body(jj, carry):
        start = pl.multiple_of((first_diag - 1 - jj) * KEY_BLOCK, KEY_BLOCK)
        return step(kn_ref[0, pl.ds(start, KEY_BLOCK), :], vn_ref[0, pl.ds(start, KEY_BLOCK), :],
                    carry, None)

    carry = lax.fori_loop(0, first_diag, new_body, carry)

    def hist_step(start, carry, vis):
        kb = kh_ref[0, pl.ds(start, KEY_BLOCK), :].astype(BF16)
        vb = vh_ref[0, pl.ds(start, KEY_BLOCK), :].astype(BF16)
        return step(kb, vb, carry, vis)

    partial = hist_valid % KEY_BLOCK
    n_full = n_hist_blocks - (1 if partial else 0)

    def hist_body(jj, carry):
        start = pl.multiple_of((n_hist_blocks - 1 - jj) * KEY_BLOCK, KEY_BLOCK)
        return hist_step(start, carry, None)

    if partial:
        assert n_hist_blocks == 1
        carry = hist_step(0, carry, col < partial)
    else:
        carry = lax.fori_loop(0, n_full, hist_body, carry)

    acc = carry[1]
    o_ref[0] = jnp.where(lane < SB_HEAD_DIM, acc[:tq], acc[tq:]).astype(o_ref.dtype)


def _cumsum_matrix():
    r = lax.broadcasted_iota(jnp.int32, (2 * KEY_BLOCK, KEY_BLOCK + LANES), 0) % KEY_BLOCK
    c = lax.broadcasted_iota(jnp.int32, (2 * KEY_BLOCK, KEY_BLOCK + LANES), 1)
    return jnp.where((c >= KEY_BLOCK) | (r > c), 1.0, 0.0).astype(BF16)


def _attention(q, k_new, v_new, k_hist, v_hist, hist_valid, shared_hist, tq):
    b, t, _ = q.shape
    tn = k_new.shape[1]
    hl = k_hist.shape[1]
    assert t % tq == 0 and tn % KEY_BLOCK == 0 and hl % KEY_BLOCK == 0 and tn >= t
    hist_map = (lambda bi, p, i: (0, 0, p)) if shared_hist else (lambda bi, p, i: (bi, 0, p))
    kern = functools.partial(_attn_kernel, tq=tq, n_hist_blocks=hl // KEY_BLOCK,
                             hist_valid=hist_valid)
    return pl.pallas_call(
        kern,
        grid=(b, HEAD_PAIRS, t // tq),
        in_specs=[
            pl.BlockSpec((1, tq, LANES), lambda bi, p, i: (bi, i, p)),
            pl.BlockSpec((1, tn, LANES), lambda bi, p, i: (bi, 0, p)),
            pl.BlockSpec((1, tn, LANES), lambda bi, p, i: (bi, 0, p)),
            pl.BlockSpec((1, hl, LANES), hist_map),
            pl.BlockSpec((1, hl, LANES), hist_map),
            _whole(),
        ],
        out_specs=pl.BlockSpec((1, tq, LANES), lambda bi, p, i: (bi, i, p)),
        out_shape=jax.ShapeDtypeStruct((b, t, ATTN_W), BF16),
        name="attn",
        compiler_params=pltpu.CompilerParams(
            dimension_semantics=("arbitrary", "arbitrary", "arbitrary"),
            vmem_limit_bytes=VMEM_LIMIT_BYTES),
    )(q, k_new, v_new, k_hist, v_hist, _cumsum_matrix())


def _post_kernel(x_ref, pooled_ref, attn_ref, lng, lnb, wg, bg, wpg, pscale, wbp, wba, wo, bo,
                 l1g, l1b, w1, b1, w2, b2, l2g, l2b, y_ref):
    h = _layer_norm(x_ref[...], lng[...], lnb[...])
    gates = _dot(h.astype(BF16), wg[...]) + bg[...]
    pooled = pooled_ref[...]
    pool_out = jnp.concatenate(
        [_dot(pooled[:, g * POOL_GC:(g + 1) * POOL_GC], wpg[g]) for g in range(len(POOL_WINDOWS))],
        axis=-1) * pscale[...]
    branch_pool = _dot(pool_out.astype(BF16), wbp[...])
    branch_attn = _dot(attn_ref[...], wba[...])
    mix = (jax.nn.sigmoid(gates[:, :D_MODEL]) * branch_pool
           + jax.nn.sigmoid(gates[:, D_MODEL:]) * branch_attn)
    h1 = _layer_norm(ALPHA * h + (_dot(mix.astype(BF16), wo[...]) + bo[...]), l1g[...], l1b[...])
    hid = jnp.square(jnp.maximum(_dot(h1.astype(BF16), w1[...]) + b1[...], 0.0))
    f = _dot(hid.astype(BF16), w2[...]) + b2[...]
    y_ref[...] = _layer_norm(ALPHA * h1 + f, l2g[...], l2b[...])


def _post(x, pooled, attn, weights, tm):
    rows = x.shape[0]
    assert rows % tm == 0
    row_spec = lambda w: pl.BlockSpec((tm, w), lambda i: (i, 0))
    return pl.pallas_call(
        _post_kernel,
        grid=(rows // tm,),
        in_specs=[row_spec(D_MODEL), row_spec(POOL_W), row_spec(ATTN_W)]
                 + [_whole() for _ in weights],
        out_specs=row_spec(D_MODEL),
        out_shape=jax.ShapeDtypeStruct((rows, D_MODEL), F32),
        name="post",
        compiler_params=pltpu.CompilerParams(
            dimension_semantics=("arbitrary",), vmem_limit_bytes=VMEM_LIMIT_BYTES),
    )(x, pooled, attn, *weights)


def _row(v):
    return v.reshape(1, -1).astype(F32)


def _pad_rows(a, rows):
    return jnp.pad(a, ((0, 0), (0, rows - a.shape[1]), (0, 0)))


def _last_rows(halo, u):
    if u.shape[1] >= POOL_STATE:
        return u[None, :, -POOL_STATE:]
    return jnp.concatenate([halo, u], axis=1)[None, :, -POOL_STATE:]


def kernel(x_prompt, x_sample, cache_k, cache_v, state_pool, meta, ln_in_g, ln_in_b, w_in, b_in,
           w_pool_grp, pool_scale, w_br_pool, w_br_attn, w_out, b_out, ln1_g, ln1_b,
           w_mlp1, b_mlp1, w_mlp2, b_mlp2, ln2_g, ln2_b):
    assert w_in.shape[0] == DEPTH
    bp, seq, _ = x_prompt.shape
    bs, dec_seq, _ = x_sample.shape
    past = cache_k.shape[2]

    lng, lnb = _row(ln_in_g), _row(ln_in_b)
    w_in_b = w_in[0].astype(BF16)
    w_uqkv, w_gate = w_in_b[:, :UQKV_W], w_in_b[:, UQKV_W:]
    b_uqkv, b_gate = _row(b_in[0, :UQKV_W]), _row(b_in[0, UQKV_W:])
    post_weights = (
        lng, lnb, w_gate, b_gate, w_pool_grp[0].astype(BF16), _row(pool_scale[0]),
        w_br_pool[0].astype(BF16), w_br_attn[0].astype(BF16), w_out[0].astype(BF16), _row(b_out[0]),
        _row(ln1_g[0]), _row(ln1_b[0]), w_mlp1[0].astype(BF16), _row(b_mlp1[0]),
        w_mlp2[0].astype(BF16), _row(b_mlp2[0]), _row(ln2_g[0]), _row(ln2_b[0]))

    inproj = functools.partial(_inproj, ln_g=lng, ln_b=lnb, w_uqkv=w_uqkv, b_uqkv=b_uqkv)
    xp = x_prompt.reshape(bp * seq, D_MODEL)
    xs = x_sample.reshape(bs * dec_seq, D_MODEL)
    u_m, _, kf_m, vf_m, kb_m, vb_m = inproj(meta.astype(F32), tm=N_META)
    u_p, q_p, kf_p, vf_p, kb_p, vb_p = inproj(xp, tm=512)
    u_s, q_s, kf_s, vf_s, kb_s, vb_s = inproj(xs, tm=512)

    u_p3 = u_p.reshape(bp, seq, POOL_W)
    pooled_p = _pool(u_p3, u_m[None], shared_halo=True)
    attn_p = _attention(
        q_p.reshape(bp, seq, ATTN_W), kb_p.reshape(bp, seq, ATTN_W), vb_p.reshape(bp, seq, ATTN_W),
        _pad_rows(kb_m[None], KEY_BLOCK), _pad_rows(vb_m[None], KEY_BLOCK),
        hist_valid=N_META, shared_hist=True, tq=128)
    y_p = _post(xp, pooled_p.reshape(bp * seq, POOL_W), attn_p.reshape(bp * seq, ATTN_W),
                post_weights, tm=512)

    u_s3 = u_s.reshape(bs, dec_seq, POOL_W)
    halo_s = jnp.pad(state_pool[0], ((0, 0), (POOL_HALO - POOL_STATE, 0), (0, 0)))
    pooled_s = _pool(u_s3, halo_s, shared_halo=False)
    tn_s = -(-dec_seq // KEY_BLOCK) * KEY_BLOCK
    attn_s = _attention(
        q_s.reshape(bs, dec_seq, ATTN_W),
        _pad_rows(kb_s.reshape(bs, dec_seq, ATTN_W), tn_s),
        _pad_rows(vb_s.reshape(bs, dec_seq, ATTN_W), tn_s),
        cache_k[0].reshape(bs, past, ATTN_W), cache_v[0].reshape(bs, past, ATTN_W),
        hist_valid=past, shared_hist=False, tq=dec_seq)
    y_s = _post(xs, pooled_s.reshape(bs * dec_seq, POOL_W), attn_s.reshape(bs * dec_seq, ATTN_W),
                post_weights, tm=512)

    def with_meta(meta_rows, frames):
        full = jnp.concatenate(
            [jnp.broadcast_to(meta_rows[None], (bp, N_META, ATTN_W)),
             frames.reshape(bp, seq, ATTN_W)], axis=1)
        return full.reshape(DEPTH, bp, N_META + seq, SB_HEADS, SB_HEAD_DIM)

    heads_s = (DEPTH, bs, dec_seq, SB_HEADS, SB_HEAD_DIM)
    return (
        y_p.reshape(bp, seq, D_MODEL),
        y_s.reshape(bs, dec_seq, D_MODEL),
        with_meta(kf_m, kf_p),
        with_meta(vf_m, vf_p),
        _last_rows(jnp.broadcast_to(u_m[None], (bp, POOL_HALO, POOL_W)), u_p3),
        kf_s.reshape(heads_s),
        vf_s.reshape(heads_s),
        _last_rows(halo_s, u_s3),
    )
```

```python
import functools

import jax
import jax.numpy as jnp
from jax import lax
from jax.experimental import pallas as pl
from jax.experimental.pallas import tpu as pltpu

D_MODEL = 1024
N_META = 16
POOL_W = D_MODEL // 2
POOL_WINDOWS = (2, 4, 8, 16)
POOL_GC = POOL_W // len(POOL_WINDOWS)
POOL_STATE = max(POOL_WINDOWS) - 1
POOL_HALO = POOL_STATE + 1
SB_HEADS = 8
SB_HEAD_DIM = 64
ATTN_W = SB_HEADS * SB_HEAD_DIM
D_FF = 4 * D_MODEL
UQKV_W = POOL_W + 3 * ATTN_W
LN_EPS = 1e-5
DEPTH = 1
ALPHA = (2.0 * DEPTH) ** 0.25
SB_SCALE = SB_HEAD_DIM ** -0.5

LANES = 128
KEY_BLOCK = LANES
HEAD_PAIRS = ATTN_W // LANES
VMEM_LIMIT_BYTES = 56 * 1024 * 1024
DEAD_LOG_WEIGHT = -110.0

F32 = jnp.float32
BF16 = jnp.bfloat16


def _layer_norm(x, g, b):
    mu = jnp.mean(x, axis=-1, keepdims=True)
    xc = x - mu
    var = jnp.mean(xc * xc, axis=-1, keepdims=True)
    return xc * lax.rsqrt(var + LN_EPS) * g + b


def _dot(a, b):
    return jnp.dot(a, b, preferred_element_type=F32)


def _whole():
    return pl.BlockSpec(memory_space=pltpu.VMEM)


def _inproj_kernel(x_ref, g_ref, b_ref, w_ref, bias_ref,
                   u_ref, q_ref, kf_ref, vf_ref, kb_ref, vb_ref):
    h = _layer_norm(x_ref[...], g_ref[...], b_ref[...])
    z = _dot(h.astype(BF16), w_ref[...]) + bias_ref[...]
    u_ref[...] = z[:, :POOL_W]
    q_ref[...] = (z[:, POOL_W:POOL_W + ATTN_W] * SB_SCALE).astype(BF16)
    k = z[:, POOL_W + ATTN_W:POOL_W + 2 * ATTN_W]
    v = z[:, POOL_W + 2 * ATTN_W:]
    kf_ref[...] = k
    vf_ref[...] = v
    kb_ref[...] = k.astype(BF16)
    vb_ref[...] = v.astype(BF16)


def _inproj(x, ln_g, ln_b, w_uqkv, b_uqkv, tm):
    rows = x.shape[0]
    assert rows % tm == 0
    row_spec = lambda w: pl.BlockSpec((tm, w), lambda i: (i, 0))
    out_shape = (
        jax.ShapeDtypeStruct((rows, POOL_W), F32),
        jax.ShapeDtypeStruct((rows, ATTN_W), BF16),
        jax.ShapeDtypeStruct((rows, ATTN_W), F32),
        jax.ShapeDtypeStruct((rows, ATTN_W), F32),
        jax.ShapeDtypeStruct((rows, ATTN_W), BF16),
        jax.ShapeDtypeStruct((rows, ATTN_W), BF16),
    )
    return pl.pallas_call(
        _inproj_kernel,
        grid=(rows // tm,),
        in_specs=[row_spec(D_MODEL), _whole(), _whole(), _whole(), _whole()],
        out_specs=tuple(row_spec(s.shape[1]) for s in out_shape),
        out_shape=out_shape,
        name="inproj",
        compiler_params=pltpu.CompilerParams(
            dimension_semantics=("arbitrary",), vmem_limit_bytes=VMEM_LIMIT_BYTES),
    )(x, ln_g, ln_b, w_uqkv, b_uqkv)


def _pool_kernel(halo_ref, u_ref, o_ref):
    t = u_ref.shape[1]
    full = jnp.concatenate([halo_ref[0], u_ref[0]], axis=0)
    outs = []
    for g, w in enumerate(POOL_WINDOWS):
        x = full[:, g * POOL_GC:(g + 1) * POOL_GC]
        s = x
        span = 1
        while span < w:
            s = s + pltpu.roll(s, span, axis=0)
            span *= 2
        outs.append(s[POOL_HALO:] * (1.0 / w) - x[POOL_HALO:])
    o_ref[0] = jnp.concatenate(outs, axis=-1).astype(o_ref.dtype)


def _pool(u, halo, shared_halo):
    b, t, _ = u.shape
    halo_map = (lambda i: (0, 0, 0)) if shared_halo else (lambda i: (i, 0, 0))
    return pl.pallas_call(
        _pool_kernel,
        grid=(b,),
        in_specs=[pl.BlockSpec((1, POOL_HALO, POOL_W), halo_map),
                  pl.BlockSpec((1, t, POOL_W), lambda i: (i, 0, 0))],
        out_specs=pl.BlockSpec((1, t, POOL_W), lambda i: (i, 0, 0)),
        out_shape=jax.ShapeDtypeStruct((b, t, POOL_W), BF16),
        name="pool",
        compiler_params=pltpu.CompilerParams(
            dimension_semantics=("arbitrary",), vmem_limit_bytes=VMEM_LIMIT_BYTES),
    )(halo, u)


def _attn_kernel(q_ref, kn_ref, vn_ref, kh_ref, vh_ref, su_ref, o_ref, *,
                 tq, n_hist_blocks, hist_valid):
    i = pl.program_id(1)
    lane = lax.broadcasted_iota(jnp.int32, (tq, LANES), 1)
    even = lane < SB_HEAD_DIM
    qs = []
    for p in range(HEAD_PAIRS):
        q = q_ref[0, :, p * LANES:(p + 1) * LANES]
        zero = jnp.zeros_like(q)
        qs.append(jnp.concatenate([jnp.where(even, q, zero), jnp.where(even, zero, q)], axis=0))
    su = su_ref[...]

    row = lax.broadcasted_iota(jnp.int32, (2 * tq, KEY_BLOCK), 0)
    col = lax.broadcasted_iota(jnp.int32, (2 * tq, KEY_BLOCK), 1)
    q_pos = jnp.where(row >= tq, row - tq, row) + i * tq

    pairs = range(HEAD_PAIRS)

    def step(k_blk, v_blk, carry, vis):
        sl = [slice(p * LANES, (p + 1) * LANES) for p in pairs]
        s = [lax.dot_general(qs[p], k_blk[:, sl[p]], (((1,), (1,)), ((), ())),
                             preferred_element_type=F32) for p in pairs]
        tl, lb = [], []
        for p in pairs:
            lk = -(jnp.maximum(s[p], 0.0) + jnp.log(1.0 + jnp.exp(-jnp.abs(s[p]))))
            lb.append(s[p] + lk)
            if vis is not None:
                lk = jnp.where(vis, lk, 0.0)
            hi = lk.astype(BF16)
            lo = (lk - hi.astype(F32)).astype(BF16)
            tl.append(_dot(jnp.concatenate([hi, lo], axis=1), su))
        out = []
        for p in pairs:
            c, acc = carry[p]
            w = jnp.exp(lb[p] + tl[p][:, :KEY_BLOCK] + c)
            if vis is not None:
                w = jnp.where(vis, w, 0.0)
            out.append((c + tl[p][:, KEY_BLOCK:], acc + _dot(w.astype(BF16), v_blk[:, sl[p]])))
        return tuple(out)

    zeros = jnp.zeros((2 * tq, LANES), F32)
    carry = tuple((zeros, zeros) for _ in range(HEAD_PAIRS))

    n_diag = max(tq // KEY_BLOCK, 1)
    first_diag = i * (tq // KEY_BLOCK) if tq >= KEY_BLOCK else (i * tq) // KEY_BLOCK
    for d in reversed(range(n_diag)):
        start = pl.multiple_of((first_diag + d) * KEY_BLOCK, KEY_BLOCK)
        vis = (first_diag + d) * KEY_BLOCK + col < q_pos
        carry = step(kn_ref[0, pl.ds(start, KEY_BLOCK), :], vn_ref[0, pl.ds(start, KEY_BLOCK), :],
                     carry, vis)

    def alive(carry):
        c_max = functools.reduce(jnp.maximum, [carry[p][0] for p in pairs])
        return (jnp.max(c_max) > DEAD_LOG_WEIGHT).astype(jnp.int32)

    def visit(n_blocks, body, live, carry):
        def cond(state):
            return jnp.logical_and(state[0] < n_blocks, state[1] > 0)

        def loop_body(state):
            new_carry = body(state[0], state[2])
            return state[0] + 1, alive(new_carry), new_carry

        _, live, carry = lax.while_loop(cond, loop_body, (jnp.int32(0), live, carry))
        return live, carry

    def new_body(jj, carry):
        start = pl.multiple_of((first_diag - 1 - jj) * KEY_BLOCK, KEY_BLOCK)
        return step(kn_ref[0, pl.ds(start, KEY_BLOCK), :], vn_ref[0, pl.ds(start, KEY_BLOCK), :],
                    carry, None)

    live, carry = visit(first_diag, new_body, alive(carry), carry)

    def hist_step(start, carry, vis):
        k_blk = kh_ref[0, pl.ds(start, KEY_BLOCK), :].astype(BF16)
        v_blk = vh_ref[0, pl.ds(start, KEY_BLOCK), :].astype(BF16)
        return step(k_blk, v_blk, carry, vis)

    partial = hist_valid % KEY_BLOCK

    def hist_body(jj, carry):
        start = pl.multiple_of((n_hist_blocks - 1 - jj) * KEY_BLOCK, KEY_BLOCK)
        return hist_step(start, carry, None)

    if partial:
        assert n_hist_blocks == 1
        _, carry = visit(1, lambda jj, carry: hist_step(0, carry, col < partial), live, carry)
    else:
        _, carry = visit(n_hist_blocks, hist_body, live, carry)

    for p in range(HEAD_PAIRS):
        acc = carry[p][1]
        o_ref[0, :, p * LANES:(p + 1) * LANES] = (
            jnp.where(even, acc[:tq], acc[tq:]).astype(o_ref.dtype))


def _cumsum_matrix():
    r = lax.broadcasted_iota(jnp.int32, (2 * KEY_BLOCK, KEY_BLOCK + LANES), 0) % KEY_BLOCK
    c = lax.broadcasted_iota(jnp.int32, (2 * KEY_BLOCK, KEY_BLOCK + LANES), 1)
    return jnp.where((c >= KEY_BLOCK) | (r > c), 1.0, 0.0).astype(BF16)


def _attention(q, k_new, v_new, k_hist, v_hist, hist_valid, shared_hist, tq):
    b, t, _ = q.shape
    tn = k_new.shape[1]
    hl = k_hist.shape[1]
    assert t % tq == 0 and tn % KEY_BLOCK == 0 and hl % KEY_BLOCK == 0 and tn >= t
    hist_map = (lambda bi, i: (0, 0, 0)) if shared_hist else (lambda bi, i: (bi, 0, 0))
    kern = functools.partial(_attn_kernel, tq=tq, n_hist_blocks=hl // KEY_BLOCK,
                             hist_valid=hist_valid)
    return pl.pallas_call(
        kern,
        grid=(b, t // tq),
        in_specs=[
            pl.BlockSpec((1, tq, ATTN_W), lambda bi, i: (bi, i, 0)),
            pl.BlockSpec((1, tn, ATTN_W), lambda bi, i: (bi, 0, 0)),
            pl.BlockSpec((1, tn, ATTN_W), lambda bi, i: (bi, 0, 0)),
            pl.BlockSpec((1, hl, ATTN_W), hist_map),
            pl.BlockSpec((1, hl, ATTN_W), hist_map),
            _whole(),
        ],
        out_specs=pl.BlockSpec((1, tq, ATTN_W), lambda bi, i: (bi, i, 0)),
        out_shape=jax.ShapeDtypeStruct((b, t, ATTN_W), BF16),
        name="attn",
        compiler_params=pltpu.CompilerParams(
            dimension_semantics=("arbitrary", "arbitrary"),
            vmem_limit_bytes=VMEM_LIMIT_BYTES),
    )(q, k_new, v_new, k_hist, v_hist, _cumsum_matrix())


def _post_kernel(x_ref, pooled_ref, attn_ref, lng, lnb, wg, bg, wpg, pscale, wbp, wba, wo, bo,
                 l1g, l1b, w1, b1, w2, b2, l2g, l2b, y_ref):
    h = _layer_norm(x_ref[...], lng[...], lnb[...])
    gates = _dot(h.astype(BF16), wg[...]) + bg[...]
    pooled = pooled_ref[...]
    pool_out = jnp.concatenate(
        [_dot(pooled[:, g * POOL_GC:(g + 1) * POOL_GC], wpg[g]) for g in range(len(POOL_WINDOWS))],
        axis=-1) * pscale[...]
    branch_pool = _dot(pool_out.astype(BF16), wbp[...])
    branch_attn = _dot(attn_ref[...], wba[...])
    mix = (jax.nn.sigmoid(gates[:, :D_MODEL]) * branch_pool
           + jax.nn.sigmoid(gates[:, D_MODEL:]) * branch_attn)
    h1 = _layer_norm(ALPHA * h + (_dot(mix.astype(BF16), wo[...]) + bo[...]), l1g[...], l1b[...])
    hid = jnp.square(jnp.maximum(_dot(h1.astype(BF16), w1[...]) + b1[...], 0.0))
    f = _dot(hid.astype(BF16), w2[...]) + b2[...]
    y_ref[...] = _layer_norm(ALPHA * h1 + f, l2g[...], l2b[...])


def _post(x, pooled, attn, weights, tm):
    rows = x.shape[0]
    assert rows % tm == 0
    row_spec = lambda w: pl.BlockSpec((tm, w), lambda i: (i, 0))
    return pl.pallas_call(
        _post_kernel,
        grid=(rows // tm,),
        in_specs=[row_spec(D_MODEL), row_spec(POOL_W), row_spec(ATTN_W)]
                 + [_whole() for _ in weights],
        out_specs=row_spec(D_MODEL),
        out_shape=jax.ShapeDtypeStruct((rows, D_MODEL), F32),
        name="post",
        compiler_params=pltpu.CompilerParams(
            dimension_semantics=("arbitrary",), vmem_limit_bytes=VMEM_LIMIT_BYTES),
    )(x, pooled, attn, *weights)


def _row(v):
    return v.reshape(1, -1).astype(F32)


def _pad_rows(a, rows):
    return jnp.pad(a, ((0, 0), (0, rows - a.shape[1]), (0, 0)))


def _last_rows(halo, u):
    if u.shape[1] >= POOL_STATE:
        return u[None, :, -POOL_STATE:]
    return jnp.concatenate([halo, u], axis=1)[None, :, -POOL_STATE:]


def kernel(x_prompt, x_sample, cache_k, cache_v, state_pool, meta, ln_in_g, ln_in_b, w_in, b_in,
           w_pool_grp, pool_scale, w_br_pool, w_br_attn, w_out, b_out, ln1_g, ln1_b,
           w_mlp1, b_mlp1, w_mlp2, b_mlp2, ln2_g, ln2_b):
    assert w_in.shape[0] == DEPTH
    bp, seq, _ = x_prompt.shape
    bs, dec_seq, _ = x_sample.shape
    past = cache_k.shape[2]

    lng, lnb = _row(ln_in_g), _row(ln_in_b)
    w_in_b = w_in[0].astype(BF16)
    w_uqkv, w_gate = w_in_b[:, :UQKV_W], w_in_b[:, UQKV_W:]
    b_uqkv, b_gate = _row(b_in[0, :UQKV_W]), _row(b_in[0, UQKV_W:])
    post_weights = (
        lng, lnb, w_gate, b_gate, w_pool_grp[0].astype(BF16), _row(pool_scale[0]),
        w_br_pool[0].astype(BF16), w_br_attn[0].astype(BF16), w_out[0].astype(BF16), _row(b_out[0]),
        _row(ln1_g[0]), _row(ln1_b[0]), w_mlp1[0].astype(BF16), _row(b_mlp1[0]),
        w_mlp2[0].astype(BF16), _row(b_mlp2[0]), _row(ln2_g[0]), _row(ln2_b[0]))

    inproj = functools.partial(_inproj, ln_g=lng, ln_b=lnb, w_uqkv=w_uqkv, b_uqkv=b_uqkv)
    xp = x_prompt.reshape(bp * seq, D_MODEL)
    xs = x_sample.reshape(bs * dec_seq, D_MODEL)
    u_m, _, kf_m, vf_m, kb_m, vb_m = inproj(meta.astype(F32), tm=N_META)
    u_p, q_p, kf_p, vf_p, kb_p, vb_p = inproj(xp, tm=512)
    u_s, q_s, kf_s, vf_s, kb_s, vb_s = inproj(xs, tm=512)

    u_p3 = u_p.reshape(bp, seq, POOL_W)
    pooled_p = _pool(u_p3, u_m[None], shared_halo=True)
    attn_p = _attention(
        q_p.reshape(bp, seq, ATTN_W), kb_p.reshape(bp, seq, ATTN_W), vb_p.reshape(bp, seq, ATTN_W),
        _pad_rows(kb_m[None], KEY_BLOCK), _pad_rows(vb_m[None], KEY_BLOCK),
        hist_valid=N_META, shared_hist=True, tq=128)
    y_p = _post(xp, pooled_p.reshape(bp * seq, POOL_W), attn_p.reshape(bp * seq, ATTN_W),
                post_weights, tm=512)

    u_s3 = u_s.reshape(bs, dec_seq, POOL_W)
    halo_s = jnp.pad(state_pool[0], ((0, 0), (POOL_HALO - POOL_STATE, 0), (0, 0)))
    pooled_s = _pool(u_s3, halo_s, shared_halo=False)
    tn_s = -(-dec_seq // KEY_BLOCK) * KEY_BLOCK
    attn_s = _attention(
        q_s.reshape(bs, dec_seq, ATTN_W),
        _pad_rows(kb_s.reshape(bs, dec_seq, ATTN_W), tn_s),
        _pad_rows(vb_s.reshape(bs, dec_seq, ATTN_W), tn_s),
        cache_k[0].reshape(bs, past, ATTN_W), cache_v[0].reshape(bs, past, ATTN_W),
        hist_valid=past, shared_hist=False, tq=dec_seq)
    y_s = _post(xs, pooled_s.reshape(bs * dec_seq, POOL_W), attn_s.reshape(bs * dec_seq, ATTN_W),
                post_weights, tm=512)

    def with_meta(meta_rows, frames):
        full = jnp.concatenate(
            [jnp.broadcast_to(meta_rows[None], (bp, N_META, ATTN_W)),
             frames.reshape(bp, seq, ATTN_W)], axis=1)
        return full.reshape(DEPTH, bp, N_META + seq, SB_HEADS, SB_HEAD_DIM)

    heads_s = (DEPTH, bs, dec_seq, SB_HEADS, SB_HEAD_DIM)
    return (
        y_p.reshape(bp, seq, D_MODEL),
        y_s.reshape(bs, dec_seq, D_MODEL),
        with_meta(kf_m, kf_p),
        with_meta(vf_m, vf_p),
        _last_rows(jnp.broadcast_to(u_m[None], (bp, POOL_HALO, POOL_W)), u_p3),
        kf_s.reshape(heads_s),
        vf_s.reshape(heads_s),
        _last_rows(halo_s, u_s3),
    )
```

```python
import functools

import jax
import jax.numpy as jnp
from jax import lax
from jax.experimental import pallas as pl
from jax.experimental.pallas import tpu as pltpu

D_MODEL = 1024
N_META = 16
POOL_W = D_MODEL // 2
POOL_WINDOWS = (2, 4, 8, 16)
POOL_GC = POOL_W // len(POOL_WINDOWS)
POOL_STATE = max(POOL_WINDOWS) - 1
POOL_HALO = POOL_STATE + 1
SB_HEADS = 8
SB_HEAD_DIM = 64
ATTN_W = SB_HEADS * SB_HEAD_DIM
D_FF = 4 * D_MODEL
UQKV_W = POOL_W + 3 * ATTN_W
LN_EPS = 1e-5
DEPTH = 1
ALPHA = (2.0 * DEPTH) ** 0.25
SB_SCALE = SB_HEAD_DIM ** -0.5

LANES = 128
KEY_BLOCK = LANES
HEAD_PAIRS = ATTN_W // LANES
VMEM_LIMIT_BYTES = 56 * 1024 * 1024
DEAD_LOG_WEIGHT = -110.0

F32 = jnp.float32
BF16 = jnp.bfloat16


def _layer_norm(x, g, b):
    mu = jnp.mean(x, axis=-1, keepdims=True)
    xc = x - mu
    var = jnp.mean(xc * xc, axis=-1, keepdims=True)
    return xc * lax.rsqrt(var + LN_EPS) * g + b


def _dot(a, b):
    return jnp.dot(a, b, preferred_element_type=F32)


def _whole():
    return pl.BlockSpec(memory_space=pltpu.VMEM)


def _inproj_kernel(x_ref, g_ref, b_ref, w_ref, bias_ref,
                   u_ref, q_ref, kf_ref, vf_ref, kb_ref, vb_ref):
    h = _layer_norm(x_ref[...], g_ref[...], b_ref[...])
    z = _dot(h.astype(BF16), w_ref[...]) + bias_ref[...]
    u_ref[...] = z[:, :POOL_W]
    q_ref[...] = (z[:, POOL_W:POOL_W + ATTN_W] * SB_SCALE).astype(BF16)
    k = z[:, POOL_W + ATTN_W:POOL_W + 2 * ATTN_W]
    v = z[:, POOL_W + 2 * ATTN_W:]
    kf_ref[...] = k
    vf_ref[...] = v
    kb_ref[...] = k.astype(BF16)
    vb_ref[...] = v.astype(BF16)


def _inproj(x, ln_g, ln_b, w_uqkv, b_uqkv, tm):
    rows = x.shape[0]
    assert rows % tm == 0
    row_spec = lambda w: pl.BlockSpec((tm, w), lambda i: (i, 0))
    out_shape = (
        jax.ShapeDtypeStruct((rows, POOL_W), F32),
        jax.ShapeDtypeStruct((rows, ATTN_W), BF16),
        jax.ShapeDtypeStruct((rows, ATTN_W), F32),
        jax.ShapeDtypeStruct((rows, ATTN_W), F32),
        jax.ShapeDtypeStruct((rows, ATTN_W), BF16),
        jax.ShapeDtypeStruct((rows, ATTN_W), BF16),
    )
    return pl.pallas_call(
        _inproj_kernel,
        grid=(rows // tm,),
        in_specs=[row_spec(D_MODEL), _whole(), _whole(), _whole(), _whole()],
        out_specs=tuple(row_spec(s.shape[1]) for s in out_shape),
        out_shape=out_shape,
        name="inproj",
        compiler_params=pltpu.CompilerParams(
            dimension_semantics=("arbitrary",), vmem_limit_bytes=VMEM_LIMIT_BYTES),
    )(x, ln_g, ln_b, w_uqkv, b_uqkv)


def _inproj_t_kernel(x_ref, g_ref, b_ref, w_uq_ref, b_uq_ref, w_kvt_ref, b_kvt_ref,
                     u_ref, q_ref, kt_ref, vt_ref, ktb_ref, vtb_ref):
    hb = _layer_norm(x_ref[...], g_ref[...], b_ref[...]).astype(BF16)
    z = _dot(hb, w_uq_ref[...]) + b_uq_ref[...]
    u_ref[...] = z[:, :POOL_W]
    q_ref[...] = (z[:, POOL_W:] * SB_SCALE).astype(BF16)
    zt = lax.dot_general(w_kvt_ref[...], hb, (((1,), (1,)), ((), ())),
                         preferred_element_type=F32) + b_kvt_ref[...]
    kt, vt = zt[:ATTN_W], zt[ATTN_W:]
    kt_ref[0] = kt
    vt_ref[0] = vt
    for c in range(ktb_ref.shape[1]):
        cols = slice(c * KEY_BLOCK, (c + 1) * KEY_BLOCK)
        ktb_ref[0, c] = kt[:, cols].astype(BF16)
        vtb_ref[0, c] = vt[:, cols].astype(BF16)


def _inproj_t(x, ln_g, ln_b, w_uq, b_uq, w_kvt, b_kvt, streams, tm):
    rows = x.shape[0]
    t = rows // streams
    assert rows == streams * t and t % tm == 0 and tm % KEY_BLOCK == 0
    per = t // tm
    kb = tm // KEY_BLOCK
    row_spec = lambda w: pl.BlockSpec((tm, w), lambda i: (i, 0))
    t_spec = pl.BlockSpec((1, ATTN_W, tm), lambda i: (i // per, 0, i % per))
    tb_spec = pl.BlockSpec((1, kb, ATTN_W, KEY_BLOCK), lambda i: (i // per, i % per, 0, 0))
    out_shape = (
        jax.ShapeDtypeStruct((rows, POOL_W), F32),
        jax.ShapeDtypeStruct((rows, ATTN_W), BF16),
        jax.ShapeDtypeStruct((streams, ATTN_W, t), F32),
        jax.ShapeDtypeStruct((streams, ATTN_W, t), F32),
        jax.ShapeDtypeStruct((streams, t // KEY_BLOCK, ATTN_W, KEY_BLOCK), BF16),
        jax.ShapeDtypeStruct((streams, t // KEY_BLOCK, ATTN_W, KEY_BLOCK), BF16),
    )
    return pl.pallas_call(
        _inproj_t_kernel,
        grid=(rows // tm,),
        in_specs=[row_spec(D_MODEL)] + [_whole()] * 6,
        out_specs=(row_spec(POOL_W), row_spec(ATTN_W), t_spec, t_spec, tb_spec, tb_spec),
        out_shape=out_shape,
        name="inproj_t",
        compiler_params=pltpu.CompilerParams(
            dimension_semantics=("arbitrary",), vmem_limit_bytes=VMEM_LIMIT_BYTES),
    )(x, ln_g, ln_b, w_uq, b_uq, w_kvt, b_kvt)


def _pool_kernel(halo_ref, u_ref, o_ref):
    t = u_ref.shape[1]
    full = jnp.concatenate([halo_ref[0], u_ref[0]], axis=0)
    outs = []
    for g, w in enumerate(POOL_WINDOWS):
        x = full[:, g * POOL_GC:(g + 1) * POOL_GC]
        s = x
        span = 1
        while span < w:
            s = s + pltpu.roll(s, span, axis=0)
            span *= 2
        outs.append(s[POOL_HALO:] * (1.0 / w) - x[POOL_HALO:])
    o_ref[0] = jnp.concatenate(outs, axis=-1).astype(o_ref.dtype)


def _pool(u, halo, shared_halo):
    b, t, _ = u.shape
    halo_map = (lambda i: (0, 0, 0)) if shared_halo else (lambda i: (i, 0, 0))
    return pl.pallas_call(
        _pool_kernel,
        grid=(b,),
        in_specs=[pl.BlockSpec((1, POOL_HALO, POOL_W), halo_map),
                  pl.BlockSpec((1, t, POOL_W), lambda i: (i, 0, 0))],
        out_specs=pl.BlockSpec((1, t, POOL_W), lambda i: (i, 0, 0)),
        out_shape=jax.ShapeDtypeStruct((b, t, POOL_W), BF16),
        name="pool",
        compiler_params=pltpu.CompilerParams(
            dimension_semantics=("arbitrary",), vmem_limit_bytes=VMEM_LIMIT_BYTES),
    )(halo, u)


def _attn_kernel(q_ref, kn_ref, vn_ref, kh_ref, vh_ref, su_ref, o_ref, *scratch,
                 tq, new_keys_t, n_hist_blocks, hist_valid, hist_dma):
    stream = pl.program_id(0)
    i = pl.program_id(1)

    def hist_copies(jj, slot):
        hbuf, hsem = scratch
        start = pl.multiple_of((n_hist_blocks - 1 - jj) * KEY_BLOCK, KEY_BLOCK)
        return [pltpu.make_async_copy(src.at[stream, :, pl.ds(start, KEY_BLOCK)],
                                      hbuf.at[slot, which], hsem.at[slot, which])
                for which, src in enumerate((kh_ref, vh_ref))]

    if hist_dma:
        for copy in hist_copies(0, 0):
            copy.start()

    lane = lax.broadcasted_iota(jnp.int32, (tq, LANES), 1)
    even = lane < SB_HEAD_DIM
    qs = []
    for p in range(HEAD_PAIRS):
        q = q_ref[0, :, p * LANES:(p + 1) * LANES]
        zero = jnp.zeros_like(q)
        qs.append(jnp.concatenate([jnp.where(even, q, zero), jnp.where(even, zero, q)], axis=0))
    su = su_ref[...]

    row = lax.broadcasted_iota(jnp.int32, (2 * tq, KEY_BLOCK), 0)
    col = lax.broadcasted_iota(jnp.int32, (2 * tq, KEY_BLOCK), 1)
    q_pos = jnp.where(row >= tq, row - tq, row) + i * tq

    pairs = range(HEAD_PAIRS)

    nt_dims = (((1,), (1,)), ((), ()))

    def step(k_blk, v_blk, carry, vis, transposed):
        sl = [slice(p * LANES, (p + 1) * LANES) for p in pairs]
        if transposed:
            s = [_dot(qs[p], k_blk[sl[p], :]) for p in pairs]
        else:
            s = [lax.dot_general(qs[p], k_blk[:, sl[p]], nt_dims, preferred_element_type=F32)
                 for p in pairs]
        tl, lb = [], []
        for p in pairs:
            lk = -(jnp.maximum(s[p], 0.0) + jnp.log(1.0 + jnp.exp(-jnp.abs(s[p]))))
            lb.append(s[p] + lk)
            if vis is not None:
                lk = jnp.where(vis, lk, 0.0)
            hi = lk.astype(BF16)
            lo = (lk - hi.astype(F32)).astype(BF16)
            tl.append(_dot(jnp.concatenate([hi, lo], axis=1), su))
        out = []
        for p in pairs:
            c, acc = carry[p]
            w = jnp.exp(lb[p] + tl[p][:, :KEY_BLOCK] + c)
            if vis is not None:
                w = jnp.where(vis, w, 0.0)
            wb = w.astype(BF16)
            if transposed:
                pv = lax.dot_general(wb, v_blk[sl[p], :], nt_dims, preferred_element_type=F32)
            else:
                pv = _dot(wb, v_blk[:, sl[p]])
            out.append((c + tl[p][:, KEY_BLOCK:], acc + pv))
        return tuple(out)

    def new_step(j, carry, vis):
        if new_keys_t:
            return step(kn_ref[0, j], vn_ref[0, j], carry, vis, True)
        start = pl.multiple_of(j * KEY_BLOCK, KEY_BLOCK)
        return step(kn_ref[0, pl.ds(start, KEY_BLOCK), :], vn_ref[0, pl.ds(start, KEY_BLOCK), :],
                    carry, vis, False)

    zeros = jnp.zeros((2 * tq, LANES), F32)
    carry = tuple((zeros, zeros) for _ in range(HEAD_PAIRS))

    n_diag = max(tq // KEY_BLOCK, 1)
    first_diag = i * (tq // KEY_BLOCK) if tq >= KEY_BLOCK else (i * tq) // KEY_BLOCK
    for d in reversed(range(n_diag)):
        carry = new_step(first_diag + d, carry, (first_diag + d) * KEY_BLOCK + col < q_pos)

    def alive(carry):
        c_max = functools.reduce(jnp.maximum, [carry[p][0] for p in pairs])
        return (jnp.max(c_max) > DEAD_LOG_WEIGHT).astype(jnp.int32)

    def visit(n_blocks, body, live, carry):
        def cond(state):
            return jnp.logical_and(state[0] < n_blocks, state[1] > 0)

        def loop_body(state):
            new_carry = body(state[0], state[2])
            return state[0] + 1, alive(new_carry), new_carry

        return lax.while_loop(cond, loop_body, (jnp.int32(0), live, carry))

    _, live, carry = visit(first_diag, lambda jj, carry: new_step(first_diag - 1 - jj, carry, None),
                           alive(carry), carry)

    partial = hist_valid % KEY_BLOCK

    if hist_dma:
        assert not partial
        hbuf = scratch[0]

        def hist_body(jj, carry):
            slot = lax.rem(jj, 2)
            for copy in hist_copies(jj, slot):
                copy.wait()

            @pl.when(jj + 1 < n_hist_blocks)
            def _():
                for copy in hist_copies(jj + 1, 1 - slot):
                    copy.start()

            return step(hbuf[slot, 0].astype(BF16), hbuf[slot, 1].astype(BF16), carry, None, True)

        visited, _, carry = visit(n_hist_blocks, hist_body, live, carry)

        @pl.when(visited < n_hist_blocks)
        def _():
            for copy in hist_copies(visited, lax.rem(visited, 2)):
                copy.wait()
    else:
        assert n_hist_blocks == 1
        vis = col < partial if partial else None
        _, _, carry = visit(
            1, lambda jj, carry: step(kh_ref[0].astype(BF16), vh_ref[0].astype(BF16), carry, vis, True),
            live, carry)

    for p in range(HEAD_PAIRS):
        acc = carry[p][1]
        o_ref[0, :, p * LANES:(p + 1) * LANES] = (
            jnp.where(even, acc[:tq], acc[tq:]).astype(o_ref.dtype))


def _cumsum_matrix():
    r = lax.broadcasted_iota(jnp.int32, (2 * KEY_BLOCK, KEY_BLOCK + LANES), 0) % KEY_BLOCK
    c = lax.broadcasted_iota(jnp.int32, (2 * KEY_BLOCK, KEY_BLOCK + LANES), 1)
    return jnp.where((c >= KEY_BLOCK) | (r > c), 1.0, 0.0).astype(BF16)


def _attention(q, k_new, v_new, k_hist, v_hist, *, new_keys_t, hist_valid, hist_dma, tq):
    b, t, _ = q.shape
    assert t % tq == 0
    if new_keys_t:
        assert k_new.shape[1] * KEY_BLOCK == t
        new_spec = pl.BlockSpec((1,) + k_new.shape[1:], lambda bi, i: (bi, 0, 0, 0))
    else:
        assert k_new.shape[1] % KEY_BLOCK == 0 and k_new.shape[1] >= t
        new_spec = pl.BlockSpec((1,) + k_new.shape[1:], lambda bi, i: (bi, 0, 0))
    hl = k_hist.shape[2]
    assert hl % KEY_BLOCK == 0 and hist_valid <= hl
    if hist_dma:
        hist_spec = pl.BlockSpec(memory_space=pl.ANY)
        scratch = [pltpu.VMEM((2, 2, ATTN_W, KEY_BLOCK), k_hist.dtype),
                   pltpu.SemaphoreType.DMA((2, 2))]
    else:
        hist_spec = pl.BlockSpec((1, ATTN_W, KEY_BLOCK), lambda bi, i: (0, 0, 0))
        scratch = []
    kern = functools.partial(_attn_kernel, tq=tq, new_keys_t=new_keys_t,
                             n_hist_blocks=hl // KEY_BLOCK, hist_valid=hist_valid,
                             hist_dma=hist_dma)
    return pl.pallas_call(
        kern,
        grid=(b, t // tq),
        in_specs=[pl.BlockSpec((1, tq, ATTN_W), lambda bi, i: (bi, i, 0)),
                  new_spec, new_spec, hist_spec, hist_spec, _whole()],
        out_specs=pl.BlockSpec((1, tq, ATTN_W), lambda bi, i: (bi, i, 0)),
        out_shape=jax.ShapeDtypeStruct((b, t, ATTN_W), BF16),
        scratch_shapes=scratch,
        name="attn",
        compiler_params=pltpu.CompilerParams(
            dimension_semantics=("arbitrary", "arbitrary"),
            vmem_limit_bytes=VMEM_LIMIT_BYTES),
    )(q, k_new, v_new, k_hist, v_hist, _cumsum_matrix())


def _post_kernel(x_ref, pooled_ref, attn_ref, lng, lnb, wg, bg, wpg, pscale, wbp, wba, wo, bo,
                 l1g, l1b, w1, b1, w2, b2, l2g, l2b, y_ref):
    h = _layer_norm(x_ref[...], lng[...], lnb[...])
    gates = _dot(h.astype(BF16), wg[...]) + bg[...]
    pooled = pooled_ref[...]
    pool_out = jnp.concatenate(
        [_dot(pooled[:, g * POOL_GC:(g + 1) * POOL_GC], wpg[g]) for g in range(len(POOL_WINDOWS))],
        axis=-1) * pscale[...]
    branch_pool = _dot(pool_out.astype(BF16), wbp[...])
    branch_attn = _dot(attn_ref[...], wba[...])
    mix = (jax.nn.sigmoid(gates[:, :D_MODEL]) * branch_pool
           + jax.nn.sigmoid(gates[:, D_MODEL:]) * branch_attn)
    h1 = _layer_norm(ALPHA * h + (_dot(mix.astype(BF16), wo[...]) + bo[...]), l1g[...], l1b[...])
    hid = jnp.square(jnp.maximum(_dot(h1.astype(BF16), w1[...]) + b1[...], 0.0))
    f = _dot(hid.astype(BF16), w2[...]) + b2[...]
    y_ref[...] = _layer_norm(ALPHA * h1 + f, l2g[...], l2b[...])


def _post(x, pooled, attn, weights, tm):
    rows = x.shape[0]
    assert rows % tm == 0
    row_spec = lambda w: pl.BlockSpec((tm, w), lambda i: (i, 0))
    return pl.pallas_call(
        _post_kernel,
        grid=(rows // tm,),
        in_specs=[row_spec(D_MODEL), row_spec(POOL_W), row_spec(ATTN_W)]
                 + [_whole() for _ in weights],
        out_specs=row_spec(D_MODEL),
        out_shape=jax.ShapeDtypeStruct((rows, D_MODEL), F32),
        name="post",
        compiler_params=pltpu.CompilerParams(
            dimension_semantics=("arbitrary",), vmem_limit_bytes=VMEM_LIMIT_BYTES),
    )(x, pooled, attn, *weights)


def _row(v):
    return v.reshape(1, -1).astype(F32)


def _pad_rows(a, rows):
    return jnp.pad(a, ((0, 0), (0, rows - a.shape[1]), (0, 0)))


def _last_rows(halo, u):
    if u.shape[1] >= POOL_STATE:
        return u[None, :, -POOL_STATE:]
    return jnp.concatenate([halo, u], axis=1)[None, :, -POOL_STATE:]


def kernel(x_prompt, x_sample, cache_k, cache_v, state_pool, meta, ln_in_g, ln_in_b, w_in, b_in,
           w_pool_grp, pool_scale, w_br_pool, w_br_attn, w_out, b_out, ln1_g, ln1_b,
           w_mlp1, b_mlp1, w_mlp2, b_mlp2, ln2_g, ln2_b):
    assert w_in.shape[0] == DEPTH
    bp, seq, _ = x_prompt.shape
    bs, dec_seq, _ = x_sample.shape
    past = cache_k.shape[2]

    lng, lnb = _row(ln_in_g), _row(ln_in_b)
    w_in_b = w_in[0].astype(BF16)
    w_uqkv, w_gate = w_in_b[:, :UQKV_W], w_in_b[:, UQKV_W:]
    b_uqkv, b_gate = _row(b_in[0, :UQKV_W]), _row(b_in[0, UQKV_W:])
    post_weights = (
        lng, lnb, w_gate, b_gate, w_pool_grp[0].astype(BF16), _row(pool_scale[0]),
        w_br_pool[0].astype(BF16), w_br_attn[0].astype(BF16), w_out[0].astype(BF16), _row(b_out[0]),
        _row(ln1_g[0]), _row(ln1_b[0]), w_mlp1[0].astype(BF16), _row(b_mlp1[0]),
        w_mlp2[0].astype(BF16), _row(b_mlp2[0]), _row(ln2_g[0]), _row(ln2_b[0]))

    inproj = functools.partial(_inproj, ln_g=lng, ln_b=lnb, w_uqkv=w_uqkv, b_uqkv=b_uqkv)
    xp = x_prompt.reshape(bp * seq, D_MODEL)
    xs = x_sample.reshape(bs * dec_seq, D_MODEL)
    u_m, _, kf_m, vf_m, kb_m, vb_m = inproj(meta.astype(F32), tm=N_META)
    uq_w = POOL_W + ATTN_W
    u_p, q_p, kt_p, vt_p, ktb_p, vtb_p = _inproj_t(
        xp, lng, lnb, w_in_b[:, :uq_w], _row(b_in[0, :uq_w]),
        w_in_b[:, uq_w:UQKV_W].T, b_in[0, uq_w:UQKV_W].reshape(-1, 1).astype(F32),
        streams=bp, tm=512)
    u_s, q_s, kf_s, vf_s, kb_s, vb_s = inproj(xs, tm=512)

    u_p3 = u_p.reshape(bp, seq, POOL_W)
    pooled_p = _pool(u_p3, u_m[None], shared_halo=True)
    meta_hist = lambda rows: jnp.pad(rows.T, ((0, 0), (0, KEY_BLOCK - N_META)))[None]
    attn_p = _attention(
        q_p.reshape(bp, seq, ATTN_W), ktb_p, vtb_p, meta_hist(kb_m), meta_hist(vb_m),
        new_keys_t=True, hist_valid=N_META, hist_dma=False, tq=128)
    y_p = _post(xp, pooled_p.reshape(bp * seq, POOL_W), attn_p.reshape(bp * seq, ATTN_W),
                post_weights, tm=512)

    u_s3 = u_s.reshape(bs, dec_seq, POOL_W)
    halo_s = jnp.pad(state_pool[0], ((0, 0), (POOL_HALO - POOL_STATE, 0), (0, 0)))
    pooled_s = _pool(u_s3, halo_s, shared_halo=False)
    tn_s = -(-dec_seq // KEY_BLOCK) * KEY_BLOCK
    keys_on_lanes = lambda c: c.transpose(0, 2, 3, 1).reshape(bs, ATTN_W, past)
    attn_s = _attention(
        q_s.reshape(bs, dec_seq, ATTN_W),
        _pad_rows(kb_s.reshape(bs, dec_seq, ATTN_W), tn_s),
        _pad_rows(vb_s.reshape(bs, dec_seq, ATTN_W), tn_s),
        keys_on_lanes(cache_k[0]), keys_on_lanes(cache_v[0]),
        new_keys_t=False, hist_valid=past, hist_dma=True, tq=dec_seq)
    y_s = _post(xs, pooled_s.reshape(bs * dec_seq, POOL_W), attn_s.reshape(bs * dec_seq, ATTN_W),
                post_weights, tm=512)

    def with_meta(meta_rows, frames_t):
        full = jnp.concatenate(
            [jnp.broadcast_to(meta_rows.T[None], (bp, ATTN_W, N_META)), frames_t], axis=2)
        full = full.reshape(DEPTH, bp, SB_HEADS, SB_HEAD_DIM, N_META + seq)
        return full.transpose(0, 1, 4, 2, 3)

    heads_s = (DEPTH, bs, dec_seq, SB_HEADS, SB_HEAD_DIM)
    return (
        y_p.reshape(bp, seq, D_MODEL),
        y_s.reshape(bs, dec_seq, D_MODEL),
        with_meta(kf_m, kt_p),
        with_meta(vf_m, vt_p),
        _last_rows(jnp.broadcast_to(u_m[None], (bp, POOL_HALO, POOL_W)), u_p3),
        kf_s.reshape(heads_s),
        vf_s.reshape(heads_s),
        _last_rows(halo_s, u_s3),
    )
```

```python
import functools

import jax
import jax.numpy as jnp
from jax import lax
from jax.experimental import pallas as pl
from jax.experimental.pallas import tpu as pltpu

D_MODEL = 1024
N_META = 16
POOL_W = D_MODEL // 2
POOL_WINDOWS = (2, 4, 8, 16)
POOL_GC = POOL_W // len(POOL_WINDOWS)
POOL_STATE = max(POOL_WINDOWS) - 1
POOL_HALO = POOL_STATE + 1
SB_HEADS = 8
SB_HEAD_DIM = 64
ATTN_W = SB_HEADS * SB_HEAD_DIM
D_FF = 4 * D_MODEL
UQKV_W = POOL_W + 3 * ATTN_W
LN_EPS = 1e-5
DEPTH = 1
ALPHA = (2.0 * DEPTH) ** 0.25
SB_SCALE = SB_HEAD_DIM ** -0.5

LANES = 128
KEY_BLOCK = LANES
HEAD_PAIRS = ATTN_W // LANES
POST_SUB = 256
INPROJ_SUB = 256
VMEM_LIMIT_BYTES = 56 * 1024 * 1024
DEAD_LOG_WEIGHT = -110.0

F32 = jnp.float32
BF16 = jnp.bfloat16


def _layer_norm(x, g, b):
    mu = jnp.mean(x, axis=-1, keepdims=True)
    xc = x - mu
    var = jnp.mean(xc * xc, axis=-1, keepdims=True)
    return xc * lax.rsqrt(var + LN_EPS) * g + b


def _dot(a, b):
    return jnp.dot(a, b, preferred_element_type=F32)


def _whole():
    return pl.BlockSpec(memory_space=pltpu.VMEM)


def _pooled(halo, u):
    full = jnp.concatenate([halo, u], axis=0)
    outs = []
    for g, w in enumerate(POOL_WINDOWS):
        x = full[:, g * POOL_GC:(g + 1) * POOL_GC]
        s = x
        span = 1
        while span < w:
            s = s + pltpu.roll(s, span, axis=0)
            span *= 2
        outs.append(s[POOL_HALO:] * (1.0 / w) - x[POOL_HALO:])
    return jnp.concatenate(outs, axis=-1)


def _inproj_kernel(x_ref, g_ref, b_ref, w_ref, bias_ref,
                   u_ref, q_ref, kf_ref, vf_ref, kb_ref, vb_ref):
    h = _layer_norm(x_ref[...], g_ref[...], b_ref[...])
    z = _dot(h.astype(BF16), w_ref[...]) + bias_ref[...]
    u_ref[...] = z[:, :POOL_W]
    q_ref[...] = (z[:, POOL_W:POOL_W + ATTN_W] * SB_SCALE).astype(BF16)
    k = z[:, POOL_W + ATTN_W:POOL_W + 2 * ATTN_W]
    v = z[:, POOL_W + 2 * ATTN_W:]
    kf_ref[...] = k
    vf_ref[...] = v
    kb_ref[...] = k.astype(BF16)
    vb_ref[...] = v.astype(BF16)


def _inproj(x, ln_g, ln_b, w_uqkv, b_uqkv, tm):
    rows = x.shape[0]
    assert rows % tm == 0
    row_spec = lambda w: pl.BlockSpec((tm, w), lambda i: (i, 0))
    out_shape = (
        jax.ShapeDtypeStruct((rows, POOL_W), F32),
        jax.ShapeDtypeStruct((rows, ATTN_W), BF16),
        jax.ShapeDtypeStruct((rows, ATTN_W), F32),
        jax.ShapeDtypeStruct((rows, ATTN_W), F32),
        jax.ShapeDtypeStruct((rows, ATTN_W), BF16),
        jax.ShapeDtypeStruct((rows, ATTN_W), BF16),
    )
    return pl.pallas_call(
        _inproj_kernel,
        grid=(rows // tm,),
        in_specs=[row_spec(D_MODEL), _whole(), _whole(), _whole(), _whole()],
        out_specs=tuple(row_spec(s.shape[1]) for s in out_shape),
        out_shape=out_shape,
        name="inproj",
        compiler_params=pltpu.CompilerParams(
            dimension_semantics=("arbitrary",), vmem_limit_bytes=VMEM_LIMIT_BYTES),
    )(x, ln_g, ln_b, w_uqkv, b_uqkv)


def _inproj_t_kernel(x_ref, g_ref, b_ref, w_uq_ref, b_uq_ref, w_kvt_ref, b_kvt_ref,
                     u_meta_ref, kt_meta_ref, vt_meta_ref,
                     pooled_ref, q_ref, u_tail_ref, kt_ref, vt_ref, ktb_ref, vtb_ref,
                     u_halo, kt_carry, vt_carry):
    j = pl.program_id(1)
    tm = x_ref.shape[0]
    lane = lax.broadcasted_iota(jnp.int32, (ATTN_W, LANES), 1)

    @pl.when(j == 0)
    def _():
        u_halo[...] = u_meta_ref[...]
        kt_carry[...] = kt_meta_ref[...]
        vt_carry[...] = vt_meta_ref[...]

    def shifted(carry, cur):
        rolled = pltpu.roll(cur, N_META, axis=1)
        out = jnp.concatenate(
            [jnp.where(lane < N_META, carry, rolled[:, :LANES]), rolled[:, LANES:]], axis=1)
        return out, rolled[:, :LANES]

    @pl.when(j < pl.num_programs(1) - 1)
    def _():
        subs = [slice(r, r + INPROJ_SUB) for r in range(0, tm, INPROJ_SUB)]
        hb = [_layer_norm(x_ref[rows, :], g_ref[...], b_ref[...]).astype(BF16) for rows in subs]
        z, zt = [], []
        for h in hb:
            z.append(_dot(h, w_uq_ref[...]))
            zt.append(lax.dot_general(w_kvt_ref[...], h, (((1,), (1,)), ((), ())),
                                      preferred_element_type=F32))
        halo, kt_c, vt_c = u_halo[...], kt_carry[...], vt_carry[...]
        for n, rows in enumerate(subs):
            zn = z[n] + b_uq_ref[...]
            u = zn[:, :POOL_W]
            q_ref[rows, :] = (zn[:, POOL_W:] * SB_SCALE).astype(BF16)
            pooled_ref[rows, :] = _pooled(halo, u).astype(BF16)
            halo = u[INPROJ_SUB - POOL_HALO:]
            ztn = zt[n] + b_kvt_ref[...]
            kt, vt = ztn[:ATTN_W], ztn[ATTN_W:]
            for c in range(INPROJ_SUB // KEY_BLOCK):
                cols = slice(c * KEY_BLOCK, (c + 1) * KEY_BLOCK)
                blk = rows.start // KEY_BLOCK + c
                ktb_ref[0, blk] = kt[:, cols].astype(BF16)
                vtb_ref[0, blk] = vt[:, cols].astype(BF16)
            kt_ref[0, :, rows], kt_c = shifted(kt_c, kt)
            vt_ref[0, :, rows], vt_c = shifted(vt_c, vt)
        u_halo[...] = halo
        u_tail_ref[0] = halo
        kt_carry[...] = kt_c
        vt_carry[...] = vt_c

    @pl.when(j == pl.num_programs(1) - 1)
    def _():
        kt_ref[0] = jnp.zeros(kt_ref.shape[1:], F32)
        vt_ref[0] = jnp.zeros(vt_ref.shape[1:], F32)
        kt_ref[0, :, :LANES] = kt_carry[...]
        vt_ref[0, :, :LANES] = vt_carry[...]


def _inproj_t(x, ln_g, ln_b, w_uq, b_uq, w_kvt, b_kvt, u_meta, kt_meta, vt_meta, streams, tm):
    rows = x.shape[0]
    t = rows // streams
    assert rows == streams * t and t % tm == 0 and POOL_HALO == N_META
    assert tm % INPROJ_SUB == 0 and INPROJ_SUB % KEY_BLOCK == 0
    per = t // tm
    kb = tm // KEY_BLOCK
    last = per - 1
    row_spec = lambda w: pl.BlockSpec((tm, w), lambda s, j: (s * per + jnp.minimum(j, last), 0))
    t_spec = pl.BlockSpec((1, ATTN_W, tm), lambda s, j: (s, 0, j))
    tb_spec = pl.BlockSpec((1, kb, ATTN_W, KEY_BLOCK),
                           lambda s, j: (s, jnp.minimum(j, last), 0, 0))
    out_shape = (
        jax.ShapeDtypeStruct((rows, POOL_W), BF16),
        jax.ShapeDtypeStruct((rows, ATTN_W), BF16),
        jax.ShapeDtypeStruct((streams, POOL_HALO, POOL_W), F32),
        jax.ShapeDtypeStruct((streams, ATTN_W, N_META + t), F32),
        jax.ShapeDtypeStruct((streams, ATTN_W, N_META + t), F32),
        jax.ShapeDtypeStruct((streams, t // KEY_BLOCK, ATTN_W, KEY_BLOCK), BF16),
        jax.ShapeDtypeStruct((streams, t // KEY_BLOCK, ATTN_W, KEY_BLOCK), BF16),
    )
    return pl.pallas_call(
        _inproj_t_kernel,
        grid=(streams, per + 1),
        in_specs=[row_spec(D_MODEL)] + [_whole()] * 9,
        out_specs=(row_spec(POOL_W), row_spec(ATTN_W),
                   pl.BlockSpec((1, POOL_HALO, POOL_W), lambda s, j: (s, 0, 0)),
                   t_spec, t_spec, tb_spec, tb_spec),
        out_shape=out_shape,
        scratch_shapes=[pltpu.VMEM((POOL_HALO, POOL_W), F32),
                        pltpu.VMEM((ATTN_W, LANES), F32), pltpu.VMEM((ATTN_W, LANES), F32)],
        name="inproj_t",
        compiler_params=pltpu.CompilerParams(
            dimension_semantics=("arbitrary", "arbitrary"), vmem_limit_bytes=VMEM_LIMIT_BYTES),
    )(x, ln_g, ln_b, w_uq, b_uq, w_kvt, b_kvt, u_meta, kt_meta, vt_meta)


def _pool_kernel(halo_ref, u_ref, o_ref):
    o_ref[0] = _pooled(halo_ref[0], u_ref[0]).astype(o_ref.dtype)


def _pool(u, halo, shared_halo):
    b, t, _ = u.shape
    halo_map = (lambda i: (0, 0, 0)) if shared_halo else (lambda i: (i, 0, 0))
    return pl.pallas_call(
        _pool_kernel,
        grid=(b,),
        in_specs=[pl.BlockSpec((1, POOL_HALO, POOL_W), halo_map),
                  pl.BlockSpec((1, t, POOL_W), lambda i: (i, 0, 0))],
        out_specs=pl.BlockSpec((1, t, POOL_W), lambda i: (i, 0, 0)),
        out_shape=jax.ShapeDtypeStruct((b, t, POOL_W), BF16),
        name="pool",
        compiler_params=pltpu.CompilerParams(
            dimension_semantics=("arbitrary",), vmem_limit_bytes=VMEM_LIMIT_BYTES),
    )(halo, u)


def _attn_kernel(q_ref, kn_ref, vn_ref, kh_ref, vh_ref, su_ref, o_ref, *scratch,
                 tq, new_keys_t, n_hist_blocks, hist_valid, hist_dma):
    stream = pl.program_id(0)
    i = pl.program_id(1)

    def hist_copies(jj, slot):
        hbuf, hsem = scratch
        start = pl.multiple_of((n_hist_blocks - 1 - jj) * KEY_BLOCK, KEY_BLOCK)
        return [pltpu.make_async_copy(src.at[stream, :, pl.ds(start, KEY_BLOCK)],
                                      hbuf.at[slot, which], hsem.at[slot, which])
                for which, src in enumerate((kh_ref, vh_ref))]

    if hist_dma:
        for copy in hist_copies(0, 0):
            copy.start()

    lane = lax.broadcasted_iota(jnp.int32, (tq, LANES), 1)
    even = lane < SB_HEAD_DIM
    qs = []
    for p in range(HEAD_PAIRS):
        q = q_ref[0, :, p * LANES:(p + 1) * LANES]
        zero = jnp.zeros_like(q)
        qs.append(jnp.concatenate([jnp.where(even, q, zero), jnp.where(even, zero, q)], axis=0))
    su = su_ref[...]

    row = lax.broadcasted_iota(jnp.int32, (2 * tq, KEY_BLOCK), 0)
    col = lax.broadcasted_iota(jnp.int32, (2 * tq, KEY_BLOCK), 1)
    q_pos = jnp.where(row >= tq, row - tq, row) + i * tq

    pairs = range(HEAD_PAIRS)

    nt_dims = (((1,), (1,)), ((), ()))

    def step(blocks, carry, transposed):
        sl = [slice(p * LANES, (p + 1) * LANES) for p in pairs]
        s = []
        for k_blk, _, _ in blocks:
            if transposed:
                s.append([_dot(qs[p], k_blk[sl[p], :]) for p in pairs])
            else:
                s.append([lax.dot_general(qs[p], k_blk[:, sl[p]], nt_dims,
                                          preferred_element_type=F32) for p in pairs])
        tl, lb = [], []
        for b, (_, _, vis) in enumerate(blocks):
            tl.append([])
            lb.append([])
            for p in pairs:
                lk = -(jnp.maximum(s[b][p], 0.0) + jnp.log(1.0 + jnp.exp(-jnp.abs(s[b][p]))))
                lb[b].append(s[b][p] + lk)
                if vis is not None:
                    lk = jnp.where(vis, lk, 0.0)
                hi = lk.astype(BF16)
                lo = (lk - hi.astype(F32)).astype(BF16)
                tl[b].append(_dot(jnp.concatenate([hi, lo], axis=1), su))
        carry = list(carry)
        for b, (_, v_blk, vis) in enumerate(blocks):
            for p in pairs:
                c, acc = carry[p]
                w = jnp.exp(lb[b][p] + tl[b][p][:, :KEY_BLOCK] + c)
                if vis is not None:
                    w = jnp.where(vis, w, 0.0)
                wb = w.astype(BF16)
                if transposed:
                    pv = lax.dot_general(wb, v_blk[sl[p], :], nt_dims, preferred_element_type=F32)
                else:
                    pv = _dot(wb, v_blk[:, sl[p]])
                carry[p] = (c + tl[b][p][:, KEY_BLOCK:], acc + pv)
        return tuple(carry)

    def new_block(j, vis):
        if new_keys_t:
            return kn_ref[0, j], vn_ref[0, j], vis
        start = pl.multiple_of(j * KEY_BLOCK, KEY_BLOCK)
        return kn_ref[0, pl.ds(start, KEY_BLOCK), :], vn_ref[0, pl.ds(start, KEY_BLOCK), :], vis

    def new_step(js, carry, vis=None):
        return step([new_block(j, vis) for j in js], carry, new_keys_t)

    zeros = jnp.zeros((2 * tq, LANES), F32)
    carry = tuple((zeros, zeros) for _ in range(HEAD_PAIRS))

    n_diag = max(tq // KEY_BLOCK, 1)
    first_diag = i * (tq // KEY_BLOCK) if tq >= KEY_BLOCK else (i * tq) // KEY_BLOCK
    for d in reversed(range(n_diag)):
        carry = new_step([first_diag + d], carry, (first_diag + d) * KEY_BLOCK + col < q_pos)

    def alive(carry):
        c_max = functools.reduce(jnp.maximum, [carry[p][0] for p in pairs])
        return (jnp.max(c_max) > DEAD_LOG_WEIGHT).astype(jnp.int32)

    def visit(n_blocks, body, live, carry):
        def cond(state):
            return jnp.logical_and(state[0] < n_blocks, state[1] > 0)

        def loop_body(state):
            new_carry = body(state[0], state[2])
            return state[0] + 1, alive(new_carry), new_carry

        return lax.while_loop(cond, loop_body, (jnp.int32(0), live, carry))

    n_double = lax.shift_right_logical(first_diag, 1)
    _, live, carry = visit(
        n_double,
        lambda jj, carry: new_step([first_diag - 1 - 2 * jj, first_diag - 2 - 2 * jj], carry),
        alive(carry), carry)
    _, live, carry = visit(jnp.bitwise_and(first_diag, 1), lambda jj, carry: new_step([0], carry),
                           live, carry)

    partial = hist_valid % KEY_BLOCK

    if hist_dma:
        assert not partial
        hbuf = scratch[0]

        def hist_body(jj, carry):
            slot = lax.rem(jj, 2)
            for copy in hist_copies(jj, slot):
                copy.wait()

            @pl.when(jj + 1 < n_hist_blocks)
            def _():
                for copy in hist_copies(jj + 1, 1 - slot):
                    copy.start()

            return step([(hbuf[slot, 0].astype(BF16), hbuf[slot, 1].astype(BF16), None)], carry, True)

        visited, _, carry = visit(n_hist_blocks, hist_body, live, carry)

        @pl.when(visited < n_hist_blocks)
        def _():
            for copy in hist_copies(visited, lax.rem(visited, 2)):
                copy.wait()
    else:
        assert n_hist_blocks == 1
        vis = col < partial if partial else None
        _, _, carry = visit(
            1, lambda jj, carry: step([(kh_ref[0].astype(BF16), vh_ref[0].astype(BF16), vis)],
                                      carry, True),
            live, carry)

    for p in range(HEAD_PAIRS):
        acc = carry[p][1]
        o_ref[0, :, p * LANES:(p + 1) * LANES] = (
            jnp.where(even, acc[:tq], acc[tq:]).astype(o_ref.dtype))


def _cumsum_matrix():
    r = lax.broadcasted_iota(jnp.int32, (2 * KEY_BLOCK, KEY_BLOCK + LANES), 0) % KEY_BLOCK
    c = lax.broadcasted_iota(jnp.int32, (2 * KEY_BLOCK, KEY_BLOCK + LANES), 1)
    return jnp.where((c >= KEY_BLOCK) | (r > c), 1.0, 0.0).astype(BF16)


def _attention(q, k_new, v_new, k_hist, v_hist, *, new_keys_t, hist_valid, hist_dma, tq):
    b, t, _ = q.shape
    assert t % tq == 0
    if new_keys_t:
        assert k_new.shape[1] * KEY_BLOCK == t
        new_spec = pl.BlockSpec((1,) + k_new.shape[1:], lambda bi, i: (bi, 0, 0, 0))
    else:
        assert k_new.shape[1] % KEY_BLOCK == 0 and k_new.shape[1] >= t
        new_spec = pl.BlockSpec((1,) + k_new.shape[1:], lambda bi, i: (bi, 0, 0))
    hl = k_hist.shape[2]
    assert hl % KEY_BLOCK == 0 and hist_valid <= hl
    if hist_dma:
        hist_spec = pl.BlockSpec(memory_space=pl.ANY)
        scratch = [pltpu.VMEM((2, 2, ATTN_W, KEY_BLOCK), k_hist.dtype),
                   pltpu.SemaphoreType.DMA((2, 2))]
    else:
        hist_spec = pl.BlockSpec((1, ATTN_W, KEY_BLOCK), lambda bi, i: (0, 0, 0))
        scratch = []
    kern = functools.partial(_attn_kernel, tq=tq, new_keys_t=new_keys_t,
                             n_hist_blocks=hl // KEY_BLOCK, hist_valid=hist_valid,
                             hist_dma=hist_dma)
    return pl.pallas_call(
        kern,
        grid=(b, t // tq),
        in_specs=[pl.BlockSpec((1, tq, ATTN_W), lambda bi, i: (bi, i, 0)),
                  new_spec, new_spec, hist_spec, hist_spec, _whole()],
        out_specs=pl.BlockSpec((1, tq, ATTN_W), lambda bi, i: (bi, i, 0)),
        out_shape=jax.ShapeDtypeStruct((b, t, ATTN_W), BF16),
        scratch_shapes=scratch,
        name="attn",
        compiler_params=pltpu.CompilerParams(
            dimension_semantics=("arbitrary", "arbitrary"),
            vmem_limit_bytes=VMEM_LIMIT_BYTES),
    )(q, k_new, v_new, k_hist, v_hist, _cumsum_matrix())


def _post_kernel(x_ref, pooled_ref, attn_ref, lng, lnb, wg, bg, wpg, pscale, wbp, wba, wo, bo,
                 l1g, l1b, w1, b1, w2, b2, l2g, l2b, y_ref):
    subs = [slice(r, r + POST_SUB) for r in range(0, x_ref.shape[0], POST_SUB)]
    groups = range(len(POOL_WINDOWS))
    h = [_layer_norm(x_ref[rows, :], lng[...], lnb[...]) for rows in subs]
    gates = [_dot(hn.astype(BF16), wg[...]) + bg[...] for hn in h]
    pool_out = []
    for rows in subs:
        pooled = pooled_ref[rows, :]
        pool_out.append(jnp.concatenate(
            [_dot(pooled[:, g * POOL_GC:(g + 1) * POOL_GC], wpg[g]) for g in groups],
            axis=-1) * pscale[...])
    branch_pool = [_dot(po.astype(BF16), wbp[...]) for po in pool_out]
    branch_attn = [_dot(attn_ref[rows, :], wba[...]) for rows in subs]
    mix = [jax.nn.sigmoid(g[:, :D_MODEL]) * bp + jax.nn.sigmoid(g[:, D_MODEL:]) * ba
           for g, bp, ba in zip(gates, branch_pool, branch_attn)]
    res = [_dot(m.astype(BF16), wo[...]) + bo[...] for m in mix]
    h1 = [_layer_norm(ALPHA * hn + r, l1g[...], l1b[...]) for hn, r in zip(h, res)]
    hid = [jnp.square(jnp.maximum(_dot(hn.astype(BF16), w1[...]) + b1[...], 0.0)) for hn in h1]
    f = [_dot(hd.astype(BF16), w2[...]) + b2[...] for hd in hid]
    for rows, hn, fn in zip(subs, h1, f):
        y_ref[rows, :] = _layer_norm(ALPHA * hn + fn, l2g[...], l2b[...])


def _post(x, pooled, attn, weights, tm):
    rows = x.shape[0]
    assert rows % tm == 0
    row_spec = lambda w: pl.BlockSpec((tm, w), lambda i: (i, 0))
    return pl.pallas_call(
        _post_kernel,
        grid=(rows // tm,),
        in_specs=[row_spec(D_MODEL), row_spec(POOL_W), row_spec(ATTN_W)]
                 + [_whole() for _ in weights],
        out_specs=row_spec(D_MODEL),
        out_shape=jax.ShapeDtypeStruct((rows, D_MODEL), F32),
        name="post",
        compiler_params=pltpu.CompilerParams(
            dimension_semantics=("arbitrary",), vmem_limit_bytes=VMEM_LIMIT_BYTES),
    )(x, pooled, attn, *weights)


def _row(v):
    return v.reshape(1, -1).astype(F32)


def _pad_rows(a, rows):
    return jnp.pad(a, ((0, 0), (0, rows - a.shape[1]), (0, 0)))


def _last_rows(halo, u):
    if u.shape[1] >= POOL_STATE:
        return u[None, :, -POOL_STATE:]
    return jnp.concatenate([halo, u], axis=1)[None, :, -POOL_STATE:]


def kernel(x_prompt, x_sample, cache_k, cache_v, state_pool, meta, ln_in_g, ln_in_b, w_in, b_in,
           w_pool_grp, pool_scale, w_br_pool, w_br_attn, w_out, b_out, ln1_g, ln1_b,
           w_mlp1, b_mlp1, w_mlp2, b_mlp2, ln2_g, ln2_b):
    assert w_in.shape[0] == DEPTH
    bp, seq, _ = x_prompt.shape
    bs, dec_seq, _ = x_sample.shape
    past = cache_k.shape[2]

    lng, lnb = _row(ln_in_g), _row(ln_in_b)
    w_in_b = w_in[0].astype(BF16)
    w_uqkv, w_gate = w_in_b[:, :UQKV_W], w_in_b[:, UQKV_W:]
    b_uqkv, b_gate = _row(b_in[0, :UQKV_W]), _row(b_in[0, UQKV_W:])
    post_weights = (
        lng, lnb, w_gate, b_gate, w_pool_grp[0].astype(BF16), _row(pool_scale[0]),
        w_br_pool[0].astype(BF16), w_br_attn[0].astype(BF16), w_out[0].astype(BF16), _row(b_out[0]),
        _row(ln1_g[0]), _row(ln1_b[0]), w_mlp1[0].astype(BF16), _row(b_mlp1[0]),
        w_mlp2[0].astype(BF16), _row(b_mlp2[0]), _row(ln2_g[0]), _row(ln2_b[0]))

    inproj = functools.partial(_inproj, ln_g=lng, ln_b=lnb, w_uqkv=w_uqkv, b_uqkv=b_uqkv)
    xp = x_prompt.reshape(bp * seq, D_MODEL)
    xs = x_sample.reshape(bs * dec_seq, D_MODEL)
    u_m, _, kf_m, vf_m, kb_m, vb_m = inproj(meta.astype(F32), tm=N_META)
    uq_w = POOL_W + ATTN_W
    meta_t = lambda rows: jnp.pad(rows.T, ((0, 0), (0, KEY_BLOCK - N_META)))
    pooled_p, q_p, u_tail_p, kt_p, vt_p, ktb_p, vtb_p = _inproj_t(
        xp, lng, lnb, w_in_b[:, :uq_w], _row(b_in[0, :uq_w]),
        w_in_b[:, uq_w:UQKV_W].T, b_in[0, uq_w:UQKV_W].reshape(-1, 1).astype(F32),
        u_m, meta_t(kf_m), meta_t(vf_m), streams=bp, tm=512)
    u_s, q_s, kf_s, vf_s, kb_s, vb_s = inproj(xs, tm=512)

    attn_p = _attention(
        q_p.reshape(bp, seq, ATTN_W), ktb_p, vtb_p, meta_t(kb_m)[None], meta_t(vb_m)[None],
        new_keys_t=True, hist_valid=N_META, hist_dma=False, tq=128)
    y_p = _post(xp, pooled_p, attn_p.reshape(bp * seq, ATTN_W), post_weights, tm=512)

    u_s3 = u_s.reshape(bs, dec_seq, POOL_W)
    halo_s = jnp.pad(state_pool[0], ((0, 0), (POOL_HALO - POOL_STATE, 0), (0, 0)))
    pooled_s = _pool(u_s3, halo_s, shared_halo=False)
    tn_s = -(-dec_seq // KEY_BLOCK) * KEY_BLOCK
    keys_on_lanes = lambda c: c.transpose(0, 2, 3, 1).reshape(bs, ATTN_W, past)
    attn_s = _attention(
        q_s.reshape(bs, dec_seq, ATTN_W),
        _pad_rows(kb_s.reshape(bs, dec_seq, ATTN_W), tn_s),
        _pad_rows(vb_s.reshape(bs, dec_seq, ATTN_W), tn_s),
        keys_on_lanes(cache_k[0]), keys_on_lanes(cache_v[0]),
        new_keys_t=False, hist_valid=past, hist_dma=True, tq=dec_seq)
    y_s = _post(xs, pooled_s.reshape(bs * dec_seq, POOL_W), attn_s.reshape(bs * dec_seq, ATTN_W),
                post_weights, tm=512)

    def seq_major(kv_t):
        full = kv_t.reshape(DEPTH, bp, SB_HEADS, SB_HEAD_DIM, N_META + seq)
        return full.transpose(0, 1, 4, 2, 3)

    assert seq >= POOL_STATE
    heads_s = (DEPTH, bs, dec_seq, SB_HEADS, SB_HEAD_DIM)
    return (
        y_p.reshape(bp, seq, D_MODEL),
        y_s.reshape(bs, dec_seq, D_MODEL),
        seq_major(kt_p),
        seq_major(vt_p),
        u_tail_p[None, :, POOL_HALO - POOL_STATE:],
        kf_s.reshape(heads_s),
        vf_s.reshape(heads_s),
        _last_rows(halo_s, u_s3),
    )
```

```python
import functools

import jax
import jax.numpy as jnp
from jax import lax
from jax.experimental import pallas as pl
from jax.experimental.pallas import tpu as pltpu

D_MODEL = 1024
N_META = 16
POOL_W = D_MODEL // 2
POOL_WINDOWS = (2, 4, 8, 16)
POOL_GC = POOL_W // len(POOL_WINDOWS)
POOL_STATE = max(POOL_WINDOWS) - 1
POOL_HALO = POOL_STATE + 1
SB_HEADS = 8
SB_HEAD_DIM = 64
ATTN_W = SB_HEADS * SB_HEAD_DIM
D_FF = 4 * D_MODEL
UQKV_W = POOL_W + 3 * ATTN_W
LN_EPS = 1e-5
DEPTH = 1
ALPHA = (2.0 * DEPTH) ** 0.25
SB_SCALE = SB_HEAD_DIM ** -0.5
LOG2_E = 1.4426950408889634
Q_SCALE = SB_SCALE * LOG2_E

LANES = 128
KEY_BLOCK = LANES
HEAD_PAIRS = ATTN_W // LANES
LEAD_EXTRA = 2
POST_SUB = 256
INPROJ_SUB = 256
VMEM_LIMIT_BYTES = 56 * 1024 * 1024
DEAD_LOG2_WEIGHT = -160.0

F32 = jnp.float32
BF16 = jnp.bfloat16


def _layer_norm(x, g, b):
    mu = jnp.mean(x, axis=-1, keepdims=True)
    xc = x - mu
    var = jnp.mean(xc * xc, axis=-1, keepdims=True)
    return xc * lax.rsqrt(var + LN_EPS) * g + b


def _dot(a, b):
    return jnp.dot(a, b, preferred_element_type=F32)


def _whole():
    return pl.BlockSpec(memory_space=pltpu.VMEM)


def _pooled(halo, u):
    full = jnp.concatenate([halo, u], axis=0)
    outs = []
    for g, w in enumerate(POOL_WINDOWS):
        x = full[:, g * POOL_GC:(g + 1) * POOL_GC]
        s = x
        span = 1
        while span < w:
            s = s + pltpu.roll(s, span, axis=0)
            span *= 2
        outs.append(s[POOL_HALO:] * (1.0 / w) - x[POOL_HALO:])
    return jnp.concatenate(outs, axis=-1)


def _inproj_kernel(x_ref, g_ref, b_ref, w_ref, bias_ref,
                   u_ref, q_ref, kf_ref, vf_ref, kb_ref, vb_ref):
    h = _layer_norm(x_ref[...], g_ref[...], b_ref[...])
    z = _dot(h.astype(BF16), w_ref[...]) + bias_ref[...]
    u_ref[...] = z[:, :POOL_W]
    q_ref[...] = (z[:, POOL_W:POOL_W + ATTN_W] * Q_SCALE).astype(BF16)
    k = z[:, POOL_W + ATTN_W:POOL_W + 2 * ATTN_W]
    v = z[:, POOL_W + 2 * ATTN_W:]
    kf_ref[...] = k
    vf_ref[...] = v
    kb_ref[...] = k.astype(BF16)
    vb_ref[...] = v.astype(BF16)


def _inproj(x, ln_g, ln_b, w_uqkv, b_uqkv, tm):
    rows = x.shape[0]
    assert rows % tm == 0
    row_spec = lambda w: pl.BlockSpec((tm, w), lambda i: (i, 0))
    out_shape = (
        jax.ShapeDtypeStruct((rows, POOL_W), F32),
        jax.ShapeDtypeStruct((rows, ATTN_W), BF16),
        jax.ShapeDtypeStruct((rows, ATTN_W), F32),
        jax.ShapeDtypeStruct((rows, ATTN_W), F32),
        jax.ShapeDtypeStruct((rows, ATTN_W), BF16),
        jax.ShapeDtypeStruct((rows, ATTN_W), BF16),
    )
    return pl.pallas_call(
        _inproj_kernel,
        grid=(rows // tm,),
        in_specs=[row_spec(D_MODEL), _whole(), _whole(), _whole(), _whole()],
        out_specs=tuple(row_spec(s.shape[1]) for s in out_shape),
        out_shape=out_shape,
        name="inproj",
        compiler_params=pltpu.CompilerParams(
            dimension_semantics=("arbitrary",), vmem_limit_bytes=VMEM_LIMIT_BYTES),
    )(x, ln_g, ln_b, w_uqkv, b_uqkv)


def _inproj_t_kernel(x_ref, g_ref, b_ref, w_uq_ref, b_uq_ref, w_kvt_ref, b_kvt_ref,
                     u_meta_ref, kt_meta_ref, vt_meta_ref,
                     pooled_ref, q_ref, u_tail_ref, kt_ref, vt_ref, ktb_ref, vtb_ref,
                     u_halo, kt_carry, vt_carry):
    j = pl.program_id(1)
    tm = x_ref.shape[0]
    lane = lax.broadcasted_iota(jnp.int32, (ATTN_W, LANES), 1)

    @pl.when(j == 0)
    def _():
        u_halo[...] = u_meta_ref[...]
        kt_carry[...] = kt_meta_ref[...]
        vt_carry[...] = vt_meta_ref[...]

    def shifted(carry, cur):
        rolled = pltpu.roll(cur, N_META, axis=1)
        out = jnp.concatenate(
            [jnp.where(lane < N_META, carry, rolled[:, :LANES]), rolled[:, LANES:]], axis=1)
        return out, rolled[:, :LANES]

    @pl.when(j < pl.num_programs(1) - 1)
    def _():
        subs = [slice(r, r + INPROJ_SUB) for r in range(0, tm, INPROJ_SUB)]
        hb = [_layer_norm(x_ref[rows, :], g_ref[...], b_ref[...]).astype(BF16) for rows in subs]
        z, zt = [], []
        for h in hb:
            z.append(_dot(h, w_uq_ref[...]))
            zt.append(lax.dot_general(w_kvt_ref[...], h, (((1,), (1,)), ((), ())),
                                      preferred_element_type=F32))
        halo, kt_c, vt_c = u_halo[...], kt_carry[...], vt_carry[...]
        for n, rows in enumerate(subs):
            zn = z[n] + b_uq_ref[...]
            u = zn[:, :POOL_W]
            q_ref[rows, :] = (zn[:, POOL_W:] * Q_SCALE).astype(BF16)
            pooled_ref[rows, :] = _pooled(halo, u).astype(BF16)
            halo = u[INPROJ_SUB - POOL_HALO:]
            ztn = zt[n] + b_kvt_ref[...]
            kt, vt = ztn[:ATTN_W], ztn[ATTN_W:]
            for c in range(INPROJ_SUB // KEY_BLOCK):
                cols = slice(c * KEY_BLOCK, (c + 1) * KEY_BLOCK)
                blk = rows.start // KEY_BLOCK + c
                ktb_ref[0, blk] = kt[:, cols].astype(BF16)
                vtb_ref[0, blk] = vt[:, cols].astype(BF16)
            kt_ref[0, :, rows], kt_c = shifted(kt_c, kt)
            vt_ref[0, :, rows], vt_c = shifted(vt_c, vt)
        u_halo[...] = halo
        u_tail_ref[0] = halo
        kt_carry[...] = kt_c
        vt_carry[...] = vt_c

    @pl.when(j == pl.num_programs(1) - 1)
    def _():
        kt_ref[0] = jnp.zeros(kt_ref.shape[1:], F32)
        vt_ref[0] = jnp.zeros(vt_ref.shape[1:], F32)
        kt_ref[0, :, :LANES] = kt_carry[...]
        vt_ref[0, :, :LANES] = vt_carry[...]


def _inproj_t(x, ln_g, ln_b, w_uq, b_uq, w_kvt, b_kvt, u_meta, kt_meta, vt_meta, streams, tm):
    rows = x.shape[0]
    t = rows // streams
    assert rows == streams * t and t % tm == 0 and POOL_HALO == N_META
    assert tm % INPROJ_SUB == 0 and INPROJ_SUB % KEY_BLOCK == 0
    per = t // tm
    kb = tm // KEY_BLOCK
    last = per - 1
    row_spec = lambda w: pl.BlockSpec((tm, w), lambda s, j: (s * per + jnp.minimum(j, last), 0))
    t_spec = pl.BlockSpec((1, ATTN_W, tm), lambda s, j: (s, 0, j))
    tb_spec = pl.BlockSpec((1, kb, ATTN_W, KEY_BLOCK),
                           lambda s, j: (s, jnp.minimum(j, last), 0, 0))
    out_shape = (
        jax.ShapeDtypeStruct((rows, POOL_W), BF16),
        jax.ShapeDtypeStruct((rows, ATTN_W), BF16),
        jax.ShapeDtypeStruct((streams, POOL_HALO, POOL_W), F32),
        jax.ShapeDtypeStruct((streams, ATTN_W, N_META + t), F32),
        jax.ShapeDtypeStruct((streams, ATTN_W, N_META + t), F32),
        jax.ShapeDtypeStruct((streams, t // KEY_BLOCK, ATTN_W, KEY_BLOCK), BF16),
        jax.ShapeDtypeStruct((streams, t // KEY_BLOCK, ATTN_W, KEY_BLOCK), BF16),
    )
    return pl.pallas_call(
        _inproj_t_kernel,
        grid=(streams, per + 1),
        in_specs=[row_spec(D_MODEL)] + [_whole()] * 9,
        out_specs=(row_spec(POOL_W), row_spec(ATTN_W),
                   pl.BlockSpec((1, POOL_HALO, POOL_W), lambda s, j: (s, 0, 0)),
                   t_spec, t_spec, tb_spec, tb_spec),
        out_shape=out_shape,
        scratch_shapes=[pltpu.VMEM((POOL_HALO, POOL_W), F32),
                        pltpu.VMEM((ATTN_W, LANES), F32), pltpu.VMEM((ATTN_W, LANES), F32)],
        name="inproj_t",
        compiler_params=pltpu.CompilerParams(
            dimension_semantics=("arbitrary", "arbitrary"), vmem_limit_bytes=VMEM_LIMIT_BYTES),
    )(x, ln_g, ln_b, w_uq, b_uq, w_kvt, b_kvt, u_meta, kt_meta, vt_meta)


def _pool_kernel(halo_ref, u_ref, o_ref):
    o_ref[0] = _pooled(halo_ref[0], u_ref[0]).astype(o_ref.dtype)


def _pool(u, halo, shared_halo):
    b, t, _ = u.shape
    halo_map = (lambda i: (0, 0, 0)) if shared_halo else (lambda i: (i, 0, 0))
    return pl.pallas_call(
        _pool_kernel,
        grid=(b,),
        in_specs=[pl.BlockSpec((1, POOL_HALO, POOL_W), halo_map),
                  pl.BlockSpec((1, t, POOL_W), lambda i: (i, 0, 0))],
        out_specs=pl.BlockSpec((1, t, POOL_W), lambda i: (i, 0, 0)),
        out_shape=jax.ShapeDtypeStruct((b, t, POOL_W), BF16),
        name="pool",
        compiler_params=pltpu.CompilerParams(
            dimension_semantics=("arbitrary",), vmem_limit_bytes=VMEM_LIMIT_BYTES),
    )(halo, u)


def _attn_kernel(q_ref, kn_ref, vn_ref, kh_ref, vh_ref, su_ref, o_ref, *scratch,
                 tq, single_q_block, new_keys_t, n_hist_blocks, hist_valid, hist_dma):
    stream = pl.program_id(0)
    i = pl.program_id(1)

    def hist_copies(jj, slot):
        hbuf, hsem = scratch
        start = pl.multiple_of((n_hist_blocks - 1 - jj) * KEY_BLOCK, KEY_BLOCK)
        return [pltpu.make_async_copy(src.at[stream, :, pl.ds(start, KEY_BLOCK)],
                                      hbuf.at[slot, which], hsem.at[slot, which])
                for which, src in enumerate((kh_ref, vh_ref))]

    if hist_dma:
        for copy in hist_copies(0, 0):
            copy.start()

    lane = lax.broadcasted_iota(jnp.int32, (tq, LANES), 1)
    even = lane < SB_HEAD_DIM
    qs = []
    for p in range(HEAD_PAIRS):
        q = q_ref[0, :, p * LANES:(p + 1) * LANES]
        zero = jnp.zeros_like(q)
        qs.append(jnp.concatenate([jnp.where(even, q, zero), jnp.where(even, zero, q)], axis=0))
    su = su_ref[...]

    row = lax.broadcasted_iota(jnp.int32, (2 * tq, KEY_BLOCK), 0)
    col = lax.broadcasted_iota(jnp.int32, (2 * tq, KEY_BLOCK), 1)
    q_pos = jnp.where(row >= tq, row - tq, row) + i * tq

    pairs = range(HEAD_PAIRS)

    nt_dims = (((1,), (1,)), ((), ()))

    def step(blocks, carry, transposed):
        sl = [slice(p * LANES, (p + 1) * LANES) for p in pairs]
        s = []
        for k_blk, _, _ in blocks:
            if transposed:
                s.append([_dot(qs[p], k_blk[sl[p], :]) for p in pairs])
            else:
                s.append([lax.dot_general(qs[p], k_blk[:, sl[p]], nt_dims,
                                          preferred_element_type=F32) for p in pairs])
        tl, lb = [], []
        for b, (_, _, vis) in enumerate(blocks):
            tl.append([])
            lb.append([])
            for p in pairs:
                z2 = s[b][p]
                sp = jnp.maximum(z2, 0.0) + jnp.log(1.0 + jnp.exp2(-jnp.abs(z2))) * LOG2_E
                lb[b].append(z2 - sp)
                if vis is not None:
                    sp = jnp.where(vis, sp, 0.0)
                hi = sp.astype(BF16)
                lo = (sp - hi.astype(F32)).astype(BF16)
                tl[b].append(_dot(jnp.concatenate([hi, lo], axis=1), su))
        carry = list(carry)
        for b, (_, v_blk, vis) in enumerate(blocks):
            for p in pairs:
                c, acc = carry[p]
                w = jnp.exp2(lb[b][p] + tl[b][p][:, :KEY_BLOCK] + c)
                if vis is not None:
                    w = jnp.where(vis, w, 0.0)
                wb = w.astype(BF16)
                if transposed:
                    pv = lax.dot_general(wb, v_blk[sl[p], :], nt_dims, preferred_element_type=F32)
                else:
                    pv = _dot(wb, v_blk[:, sl[p]])
                carry[p] = (c + tl[b][p][:, KEY_BLOCK:], acc + pv)
        return tuple(carry)

    def new_block(j, vis):
        if new_keys_t:
            return kn_ref[0, j], vn_ref[0, j], vis
        start = pl.multiple_of(j * KEY_BLOCK, KEY_BLOCK)
        return kn_ref[0, pl.ds(start, KEY_BLOCK), :], vn_ref[0, pl.ds(start, KEY_BLOCK), :], vis

    def new_step(js, carry, vis=None):
        return step([new_block(j, vis) for j in js], carry, new_keys_t)

    zeros = jnp.zeros((2 * tq, LANES), F32)
    carry = tuple((zeros, zeros) for _ in range(HEAD_PAIRS))

    n_diag = max(tq // KEY_BLOCK, 1)
    if single_q_block:
        first_diag = 0
    else:
        first_diag = i * (tq // KEY_BLOCK) if tq >= KEY_BLOCK else (i * tq) // KEY_BLOCK

    def lead(extra):
        blocks = [new_block(first_diag + d, (first_diag + d) * KEY_BLOCK + col < q_pos)
                  for d in reversed(range(n_diag))]
        blocks += [new_block(first_diag - 1 - e, None) for e in range(extra)]
        return step(blocks, carry, new_keys_t)

    if single_q_block:
        carry, n_rest = lead(0), 0
    else:
        has_extra = first_diag >= LEAD_EXTRA
        carry = lax.cond(has_extra, lambda: lead(LEAD_EXTRA), lambda: lead(0))
        n_rest = first_diag - jnp.where(has_extra, LEAD_EXTRA, 0)

    def alive(carry):
        c_max = functools.reduce(jnp.maximum, [carry[p][0] for p in pairs])
        return (jnp.max(c_max) > DEAD_LOG2_WEIGHT).astype(jnp.int32)

    def visit(n_blocks, body, live, carry):
        def cond(state):
            return jnp.logical_and(state[0] < n_blocks, state[1] > 0)

        def loop_body(state):
            new_carry = body(state[0], state[2])
            return state[0] + 1, alive(new_carry), new_carry

        return lax.while_loop(cond, loop_body, (jnp.int32(0), live, carry))

    live = alive(carry)
    if not single_q_block:
        _, live, carry = visit(
            lax.shift_right_logical(n_rest, 1),
            lambda jj, carry: new_step([n_rest - 1 - 2 * jj, n_rest - 2 - 2 * jj], carry),
            live, carry)
        _, live, carry = visit(jnp.bitwise_and(n_rest, 1), lambda jj, carry: new_step([0], carry),
                               live, carry)

    partial = hist_valid % KEY_BLOCK

    if hist_dma:
        assert not partial
        hbuf = scratch[0]

        def hist_body(jj, carry):
            slot = lax.rem(jj, 2)
            for copy in hist_copies(jj, slot):
                copy.wait()

            @pl.when(jj + 1 < n_hist_blocks)
            def _():
                for copy in hist_copies(jj + 1, 1 - slot):
                    copy.start()

            return step([(hbuf[slot, 0].astype(BF16), hbuf[slot, 1].astype(BF16), None)], carry, True)

        visited, _, carry = visit(n_hist_blocks, hist_body, live, carry)

        @pl.when(visited < n_hist_blocks)
        def _():
            for copy in hist_copies(visited, lax.rem(visited, 2)):
                copy.wait()
    else:
        assert n_hist_blocks == 1
        vis = col < partial if partial else None
        _, _, carry = visit(
            1, lambda jj, carry: step([(kh_ref[0].astype(BF16), vh_ref[0].astype(BF16), vis)],
                                      carry, True),
            live, carry)

    for p in range(HEAD_PAIRS):
        acc = carry[p][1]
        o_ref[0, :, p * LANES:(p + 1) * LANES] = (
            jnp.where(even, acc[:tq], acc[tq:]).astype(o_ref.dtype))


def _cumsum_matrix():
    r = lax.broadcasted_iota(jnp.int32, (2 * KEY_BLOCK, KEY_BLOCK + LANES), 0) % KEY_BLOCK
    c = lax.broadcasted_iota(jnp.int32, (2 * KEY_BLOCK, KEY_BLOCK + LANES), 1)
    return jnp.where((c >= KEY_BLOCK) | (r > c), -1.0, 0.0).astype(BF16)


def _attention(q, k_new, v_new, k_hist, v_hist, *, new_keys_t, hist_valid, hist_dma, tq):
    b, t, _ = q.shape
    assert t % tq == 0
    if new_keys_t:
        assert k_new.shape[1] * KEY_BLOCK == t
        new_spec = pl.BlockSpec((1,) + k_new.shape[1:], lambda bi, i: (bi, 0, 0, 0))
    else:
        assert k_new.shape[1] % KEY_BLOCK == 0 and k_new.shape[1] >= t
        new_spec = pl.BlockSpec((1,) + k_new.shape[1:], lambda bi, i: (bi, 0, 0))
    hl = k_hist.shape[2]
    assert hl % KEY_BLOCK == 0 and hist_valid <= hl
    if hist_dma:
        hist_spec = pl.BlockSpec(memory_space=pl.ANY)
        scratch = [pltpu.VMEM((2, 2, ATTN_W, KEY_BLOCK), k_hist.dtype),
                   pltpu.SemaphoreType.DMA((2, 2))]
    else:
        hist_spec = pl.BlockSpec((1, ATTN_W, KEY_BLOCK), lambda bi, i: (0, 0, 0))
        scratch = []
    kern = functools.partial(_attn_kernel, tq=tq, single_q_block=(t == tq), new_keys_t=new_keys_t,
                             n_hist_blocks=hl // KEY_BLOCK, hist_valid=hist_valid,
                             hist_dma=hist_dma)
    return pl.pallas_call(
        kern,
        grid=(b, t // tq),
        in_specs=[pl.BlockSpec((1, tq, ATTN_W), lambda bi, i: (bi, i, 0)),
                  new_spec, new_spec, hist_spec, hist_spec, _whole()],
        out_specs=pl.BlockSpec((1, tq, ATTN_W), lambda bi, i: (bi, i, 0)),
        out_shape=jax.ShapeDtypeStruct((b, t, ATTN_W), BF16),
        scratch_shapes=scratch,
        name="attn",
        compiler_params=pltpu.CompilerParams(
            dimension_semantics=("arbitrary", "arbitrary"),
            vmem_limit_bytes=VMEM_LIMIT_BYTES),
    )(q, k_new, v_new, k_hist, v_hist, _cumsum_matrix())


def _post_kernel(x_ref, pooled_ref, attn_ref, lng, lnb, wg, bg, wpg, pscale, wbp, wba, wo, bo,
                 l1g, l1b, w1, b1, w2, b2, l2g, l2b, y_ref):
    subs = [slice(r, r + POST_SUB) for r in range(0, x_ref.shape[0], POST_SUB)]
    groups = range(len(POOL_WINDOWS))
    h = [_layer_norm(x_ref[rows, :], lng[...], lnb[...]) for rows in subs]
    gates = [_dot(hn.astype(BF16), wg[...]) + bg[...] for hn in h]
    pool_out = []
    for rows in subs:
        pooled = pooled_ref[rows, :]
        pool_out.append(jnp.concatenate(
            [_dot(pooled[:, g * POOL_GC:(g + 1) * POOL_GC], wpg[g]) for g in groups],
            axis=-1) * pscale[...])
    branch_pool = [_dot(po.astype(BF16), wbp[...]) for po in pool_out]
    branch_attn = [_dot(attn_ref[rows, :], wba[...]) for rows in subs]
    mix = [jax.nn.sigmoid(g[:, :D_MODEL]) * bp + jax.nn.sigmoid(g[:, D_MODEL:]) * ba
           for g, bp, ba in zip(gates, branch_pool, branch_attn)]
    res = [_dot(m.astype(BF16), wo[...]) + bo[...] for m in mix]
    h1 = [_layer_norm(ALPHA * hn + r, l1g[...], l1b[...]) for hn, r in zip(h, res)]
    hid = [jnp.square(jnp.maximum(_dot(hn.astype(BF16), w1[...]) + b1[...], 0.0)) for hn in h1]
    f = [_dot(hd.astype(BF16), w2[...]) + b2[...] for hd in hid]
    for rows, hn, fn in zip(subs, h1, f):
        y_ref[rows, :] = _layer_norm(ALPHA * hn + fn, l2g[...], l2b[...])


def _post(x, pooled, attn, weights, tm):
    rows = x.shape[0]
    assert rows % tm == 0
    row_spec = lambda w: pl.BlockSpec((tm, w), lambda i: (i, 0))
    return pl.pallas_call(
        _post_kernel,
        grid=(rows // tm,),
        in_specs=[row_spec(D_MODEL), row_spec(POOL_W), row_spec(ATTN_W)]
                 + [_whole() for _ in weights],
        out_specs=row_spec(D_MODEL),
        out_shape=jax.ShapeDtypeStruct((rows, D_MODEL), F32),
        name="post",
        compiler_params=pltpu.CompilerParams(
            dimension_semantics=("arbitrary",), vmem_limit_bytes=VMEM_LIMIT_BYTES),
    )(x, pooled, attn, *weights)


def _row(v):
    return v.reshape(1, -1).astype(F32)


def _pad_rows(a, rows):
    return jnp.pad(a, ((0, 0), (0, rows - a.shape[1]), (0, 0)))


def _last_rows(halo, u):
    if u.shape[1] >= POOL_STATE:
        return u[None, :, -POOL_STATE:]
    return jnp.concatenate([halo, u], axis=1)[None, :, -POOL_STATE:]


def kernel(x_prompt, x_sample, cache_k, cache_v, state_pool, meta, ln_in_g, ln_in_b, w_in, b_in,
           w_pool_grp, pool_scale, w_br_pool, w_br_attn, w_out, b_out, ln1_g, ln1_b,
           w_mlp1, b_mlp1, w_mlp2, b_mlp2, ln2_g, ln2_b):
    assert w_in.shape[0] == DEPTH
    bp, seq, _ = x_prompt.shape
    bs, dec_seq, _ = x_sample.shape
    past = cache_k.shape[2]

    lng, lnb = _row(ln_in_g), _row(ln_in_b)
    w_in_b = w_in[0].astype(BF16)
    w_uqkv, w_gate = w_in_b[:, :UQKV_W], w_in_b[:, UQKV_W:]
    b_uqkv, b_gate = _row(b_in[0, :UQKV_W]), _row(b_in[0, UQKV_W:])
    post_weights = (
        lng, lnb, w_gate, b_gate, w_pool_grp[0].astype(BF16), _row(pool_scale[0]),
        w_br_pool[0].astype(BF16), w_br_attn[0].astype(BF16), w_out[0].astype(BF16), _row(b_out[0]),
        _row(ln1_g[0]), _row(ln1_b[0]), w_mlp1[0].astype(BF16), _row(b_mlp1[0]),
        w_mlp2[0].astype(BF16), _row(b_mlp2[0]), _row(ln2_g[0]), _row(ln2_b[0]))

    inproj = functools.partial(_inproj, ln_g=lng, ln_b=lnb, w_uqkv=w_uqkv, b_uqkv=b_uqkv)
    xp = x_prompt.reshape(bp * seq, D_MODEL)
    xs = x_sample.reshape(bs * dec_seq, D_MODEL)
    u_m, _, kf_m, vf_m, kb_m, vb_m = inproj(meta.astype(F32), tm=N_META)
    uq_w = POOL_W + ATTN_W
    meta_t = lambda rows: jnp.pad(rows.T, ((0, 0), (0, KEY_BLOCK - N_META)))
    pooled_p, q_p, u_tail_p, kt_p, vt_p, ktb_p, vtb_p = _inproj_t(
        xp, lng, lnb, w_in_b[:, :uq_w], _row(b_in[0, :uq_w]),
        w_in_b[:, uq_w:UQKV_W].T, b_in[0, uq_w:UQKV_W].reshape(-1, 1).astype(F32),
        u_m, meta_t(kf_m), meta_t(vf_m), streams=bp, tm=512)
    u_s, q_s, kf_s, vf_s, kb_s, vb_s = inproj(xs, tm=512)

    attn_p = _attention(
        q_p.reshape(bp, seq, ATTN_W), ktb_p, vtb_p, meta_t(kb_m)[None], meta_t(vb_m)[None],
        new_keys_t=True, hist_valid=N_META, hist_dma=False, tq=128)
    y_p = _post(xp, pooled_p, attn_p.reshape(bp * seq, ATTN_W), post_weights, tm=512)

    u_s3 = u_s.reshape(bs, dec_seq, POOL_W)
    halo_s = jnp.pad(state_pool[0], ((0, 0), (POOL_HALO - POOL_STATE, 0), (0, 0)))
    pooled_s = _pool(u_s3, halo_s, shared_halo=False)
    tn_s = -(-dec_seq // KEY_BLOCK) * KEY_BLOCK
    keys_on_lanes = lambda c: c.transpose(0, 2, 3, 1).reshape(bs, ATTN_W, past)
    attn_s = _attention(
        q_s.reshape(bs, dec_seq, ATTN_W),
        _pad_rows(kb_s.reshape(bs, dec_seq, ATTN_W), tn_s),
        _pad_rows(vb_s.reshape(bs, dec_seq, ATTN_W), tn_s),
        keys_on_lanes(cache_k[0]), keys_on_lanes(cache_v[0]),
        new_keys_t=False, hist_valid=past, hist_dma=True, tq=dec_seq)
    y_s = _post(xs, pooled_s.reshape(bs * dec_seq, POOL_W), attn_s.reshape(bs * dec_seq, ATTN_W),
                post_weights, tm=512)

    def seq_major(kv_t):
        full = kv_t.reshape(DEPTH, bp, SB_HEADS, SB_HEAD_DIM, N_META + seq)
        return full.transpose(0, 1, 4, 2, 3)

    assert seq >= POOL_STATE
    heads_s = (DEPTH, bs, dec_seq, SB_HEADS, SB_HEAD_DIM)
    return (
        y_p.reshape(bp, seq, D_MODEL),
        y_s.reshape(bs, dec_seq, D_MODEL),
        seq_major(kt_p),
        seq_major(vt_p),
        u_tail_p[None, :, POOL_HALO - POOL_STATE:],
        kf_s.reshape(heads_s),
        vf_s.reshape(heads_s),
        _last_rows(halo_s, u_s3),
    )
```

```python
import functools

import jax
import jax.numpy as jnp
from jax import lax
from jax.experimental import pallas as pl
from jax.experimental.pallas import tpu as pltpu

D_MODEL = 1024
N_META = 16
POOL_W = D_MODEL // 2
POOL_WINDOWS = (2, 4, 8, 16)
POOL_GC = POOL_W // len(POOL_WINDOWS)
POOL_STATE = max(POOL_WINDOWS) - 1
POOL_HALO = POOL_STATE + 1
SB_HEADS = 8
SB_HEAD_DIM = 64
ATTN_W = SB_HEADS * SB_HEAD_DIM
D_FF = 4 * D_MODEL
UQ_W = POOL_W + ATTN_W
UQKV_W = POOL_W + 3 * ATTN_W
LN_EPS = 1e-5
DEPTH = 1
ALPHA = (2.0 * DEPTH) ** 0.25
SB_SCALE = SB_HEAD_DIM ** -0.5
LOG2_E = 1.4426950408889634
Q_SCALE = SB_SCALE * LOG2_E

LANES = 128
KEY_BLOCK = LANES
HEAD_PAIRS = ATTN_W // LANES
HIST_SLOTS = 3
LEAD_EXTRA = 2
POST_SUB = 256
INPROJ_SUB = 256
VMEM_LIMIT_BYTES = 56 * 1024 * 1024
DEAD_LOG2_WEIGHT = -160.0

F32 = jnp.float32
BF16 = jnp.bfloat16


def _layer_norm(x, g, b):
    mu = jnp.mean(x, axis=-1, keepdims=True)
    xc = x - mu
    var = jnp.mean(xc * xc, axis=-1, keepdims=True)
    return xc * lax.rsqrt(var + LN_EPS) * g + b


def _dot(a, b):
    return jnp.dot(a, b, preferred_element_type=F32)


def _whole():
    return pl.BlockSpec(memory_space=pltpu.VMEM)


def _pooled(halo, u):
    full = jnp.concatenate([halo, u], axis=0)
    outs = []
    for g, w in enumerate(POOL_WINDOWS):
        x = full[:, g * POOL_GC:(g + 1) * POOL_GC]
        s = x
        span = 1
        while span < w:
            s = s + pltpu.roll(s, span, axis=0)
            span *= 2
        outs.append(s[POOL_HALO:] * (1.0 / w) - x[POOL_HALO:])
    return jnp.concatenate(outs, axis=-1)


def _inproj_kernel(x_ref, halo_ref, g_ref, b_ref, w_ref, bias_ref,
                   u_ref, pooled_ref, q_ref, kf_ref, vf_ref, kb_ref, vb_ref):
    h = _layer_norm(x_ref[...], g_ref[...], b_ref[...])
    z = _dot(h.astype(BF16), w_ref[:, :UQKV_W]) + bias_ref[:, :UQKV_W]
    u = z[:, :POOL_W]
    u_ref[...] = u
    t = x_ref.shape[0] // halo_ref.shape[0]
    for s in range(halo_ref.shape[0]):
        rows = slice(s * t, (s + 1) * t)
        pooled_ref[rows, :] = _pooled(halo_ref[s], u[rows]).astype(BF16)
    q_ref[...] = (z[:, POOL_W:POOL_W + ATTN_W] * Q_SCALE).astype(BF16)
    k = z[:, POOL_W + ATTN_W:POOL_W + 2 * ATTN_W]
    v = z[:, POOL_W + 2 * ATTN_W:]
    kf_ref[...] = k
    vf_ref[...] = v
    kb_ref[...] = k.astype(BF16)
    vb_ref[...] = v.astype(BF16)


def _inproj(x, halo, ln_g, ln_b, w_in_b, b_in_row, tm):
    rows = x.shape[0]
    t = rows // halo.shape[0]
    assert rows % tm == 0 and tm % t == 0 and rows == t * halo.shape[0]
    row_spec = lambda w: pl.BlockSpec((tm, w), lambda i: (i, 0))
    out_shape = (
        jax.ShapeDtypeStruct((rows, POOL_W), F32),
        jax.ShapeDtypeStruct((rows, POOL_W), BF16),
        jax.ShapeDtypeStruct((rows, ATTN_W), BF16),
        jax.ShapeDtypeStruct((rows, ATTN_W), F32),
        jax.ShapeDtypeStruct((rows, ATTN_W), F32),
        jax.ShapeDtypeStruct((rows, ATTN_W), BF16),
        jax.ShapeDtypeStruct((rows, ATTN_W), BF16),
    )
    return pl.pallas_call(
        _inproj_kernel,
        grid=(rows // tm,),
        in_specs=[row_spec(D_MODEL),
                  pl.BlockSpec((tm // t, POOL_HALO, POOL_W), lambda i: (i, 0, 0)),
                  _whole(), _whole(), _whole(), _whole()],
        out_specs=tuple(row_spec(s.shape[1]) for s in out_shape),
        out_shape=out_shape,
        name="inproj",
        compiler_params=pltpu.CompilerParams(
            dimension_semantics=("arbitrary",), vmem_limit_bytes=VMEM_LIMIT_BYTES),
    )(x, halo, ln_g, ln_b, w_in_b, b_in_row)


def _inproj_t_kernel(x_ref, g_ref, b_ref, w_uq_ref, b_uq_ref, w_kvt_ref, b_kvt_ref,
                     u_meta_ref, kt_meta_ref, vt_meta_ref,
                     pooled_ref, q_ref, u_tail_ref, kt_ref, vt_ref, ktb_ref, vtb_ref,
                     u_halo, kt_carry, vt_carry):
    j = pl.program_id(1)
    tm = x_ref.shape[0]
    lane = lax.broadcasted_iota(jnp.int32, (ATTN_W, LANES), 1)

    @pl.when(j == 0)
    def _():
        u_halo[...] = u_meta_ref[...]
        kt_carry[...] = kt_meta_ref[...]
        vt_carry[...] = vt_meta_ref[...]

    def shifted(carry, cur):
        rolled = pltpu.roll(cur, N_META, axis=1)
        out = jnp.concatenate(
            [jnp.where(lane < N_META, carry, rolled[:, :LANES]), rolled[:, LANES:]], axis=1)
        return out, rolled[:, :LANES]

    @pl.when(j < pl.num_programs(1) - 1)
    def _():
        subs = [slice(r, r + INPROJ_SUB) for r in range(0, tm, INPROJ_SUB)]
        hb = [_layer_norm(x_ref[rows, :], g_ref[...], b_ref[...]).astype(BF16) for rows in subs]
        z, zt = [], []
        for h in hb:
            z.append(_dot(h, w_uq_ref[:, :UQ_W]))
            zt.append(lax.dot_general(w_kvt_ref[...], h, (((1,), (1,)), ((), ())),
                                      preferred_element_type=F32))
        halo, kt_c, vt_c = u_halo[...], kt_carry[...], vt_carry[...]
        for n, rows in enumerate(subs):
            zn = z[n] + b_uq_ref[:, :UQ_W]
            u = zn[:, :POOL_W]
            q_ref[rows, :] = (zn[:, POOL_W:] * Q_SCALE).astype(BF16)
            pooled_ref[rows, :] = _pooled(halo, u).astype(BF16)
            halo = u[INPROJ_SUB - POOL_HALO:]
            ztn = zt[n] + b_kvt_ref[...]
            kt, vt = ztn[:ATTN_W], ztn[ATTN_W:]
            for c in range(INPROJ_SUB // KEY_BLOCK):
                cols = slice(c * KEY_BLOCK, (c + 1) * KEY_BLOCK)
                blk = rows.start // KEY_BLOCK + c
                ktb_ref[0, blk] = kt[:, cols].astype(BF16)
                vtb_ref[0, blk] = vt[:, cols].astype(BF16)
            kt_ref[0, :, rows], kt_c = shifted(kt_c, kt)
            vt_ref[0, :, rows], vt_c = shifted(vt_c, vt)
        u_halo[...] = halo
        u_tail_ref[0] = halo
        kt_carry[...] = kt_c
        vt_carry[...] = vt_c

    @pl.when(j == pl.num_programs(1) - 1)
    def _():
        kt_ref[0] = jnp.zeros(kt_ref.shape[1:], F32)
        vt_ref[0] = jnp.zeros(vt_ref.shape[1:], F32)
        kt_ref[0, :, :LANES] = kt_carry[...]
        vt_ref[0, :, :LANES] = vt_carry[...]


def _inproj_t(x, ln_g, ln_b, w_uq, b_uq, w_kvt, b_kvt, u_meta, kt_meta, vt_meta, streams, tm):
    rows = x.shape[0]
    t = rows // streams
    assert rows == streams * t and t % tm == 0 and POOL_HALO == N_META
    assert tm % INPROJ_SUB == 0 and INPROJ_SUB % KEY_BLOCK == 0
    per = t // tm
    kb = tm // KEY_BLOCK
    last = per - 1
    row_spec = lambda w: pl.BlockSpec((tm, w), lambda s, j: (s * per + jnp.minimum(j, last), 0))
    t_spec = pl.BlockSpec((1, ATTN_W, tm), lambda s, j: (s, 0, j))
    tb_spec = pl.BlockSpec((1, kb, ATTN_W, KEY_BLOCK),
                           lambda s, j: (s, jnp.minimum(j, last), 0, 0))
    out_shape = (
        jax.ShapeDtypeStruct((rows, POOL_W), BF16),
        jax.ShapeDtypeStruct((rows, ATTN_W), BF16),
        jax.ShapeDtypeStruct((streams, POOL_HALO, POOL_W), F32),
        jax.ShapeDtypeStruct((streams, ATTN_W, N_META + t), F32),
        jax.ShapeDtypeStruct((streams, ATTN_W, N_META + t), F32),
        jax.ShapeDtypeStruct((streams, t // KEY_BLOCK, ATTN_W, KEY_BLOCK), BF16),
        jax.ShapeDtypeStruct((streams, t // KEY_BLOCK, ATTN_W, KEY_BLOCK), BF16),
    )
    return pl.pallas_call(
        _inproj_t_kernel,
        grid=(streams, per + 1),
        in_specs=[row_spec(D_MODEL)] + [_whole()] * 9,
        out_specs=(row_spec(POOL_W), row_spec(ATTN_W),
                   pl.BlockSpec((1, POOL_HALO, POOL_W), lambda s, j: (s, 0, 0)),
                   t_spec, t_spec, tb_spec, tb_spec),
        out_shape=out_shape,
        scratch_shapes=[pltpu.VMEM((POOL_HALO, POOL_W), F32),
                        pltpu.VMEM((ATTN_W, LANES), F32), pltpu.VMEM((ATTN_W, LANES), F32)],
        name="inproj_t",
        compiler_params=pltpu.CompilerParams(
            dimension_semantics=("arbitrary", "arbitrary"), vmem_limit_bytes=VMEM_LIMIT_BYTES),
    )(x, ln_g, ln_b, w_uq, b_uq, w_kvt, b_kvt, u_meta, kt_meta, vt_meta)


def _attn_kernel(q_ref, kn_ref, vn_ref, kh_ref, vh_ref, su_ref, o_ref, *scratch,
                 tq, single_q_block, new_keys_t, n_hist_blocks, hist_valid, hist_dma):
    stream = pl.program_id(0)
    i = pl.program_id(1)

    def hist_copies(jj):
        hbuf, hsem = scratch
        slot = lax.rem(jj, HIST_SLOTS)
        start = pl.multiple_of((n_hist_blocks - 1 - jj) * KEY_BLOCK, KEY_BLOCK)
        return [pltpu.make_async_copy(src.at[stream, :, pl.ds(start, KEY_BLOCK)],
                                      hbuf.at[slot, which], hsem.at[slot, which])
                for which, src in enumerate((kh_ref, vh_ref))]

    if hist_dma:
        for jj in range(min(HIST_SLOTS - 1, n_hist_blocks)):
            for copy in hist_copies(jnp.int32(jj)):
                copy.start()

    lane = lax.broadcasted_iota(jnp.int32, (tq, LANES), 1)
    even = lane < SB_HEAD_DIM
    qs = []
    for p in range(HEAD_PAIRS):
        q = q_ref[0, :, p * LANES:(p + 1) * LANES]
        zero = jnp.zeros_like(q)
        qs.append(jnp.concatenate([jnp.where(even, q, zero), jnp.where(even, zero, q)], axis=0))
    su = su_ref[...]

    row = lax.broadcasted_iota(jnp.int32, (2 * tq, KEY_BLOCK), 0)
    col = lax.broadcasted_iota(jnp.int32, (2 * tq, KEY_BLOCK), 1)
    q_pos = jnp.where(row >= tq, row - tq, row) + i * tq

    pairs = range(HEAD_PAIRS)

    nt_dims = (((1,), (1,)), ((), ()))

    def step(blocks, carry, transposed):
        sl = [slice(p * LANES, (p + 1) * LANES) for p in pairs]
        s = []
        for k_blk, _, _ in blocks:
            if transposed:
                s.append([_dot(qs[p], k_blk[sl[p], :]) for p in pairs])
            else:
                s.append([lax.dot_general(qs[p], k_blk[:, sl[p]], nt_dims,
                                          preferred_element_type=F32) for p in pairs])
        tl, lb = [], []
        for b, (_, _, vis) in enumerate(blocks):
            tl.append([])
            lb.append([])
            for p in pairs:
                z2 = s[b][p]
                sp = jnp.maximum(z2, 0.0) + jnp.log(1.0 + jnp.exp2(-jnp.abs(z2))) * LOG2_E
                lb[b].append(z2 - sp)
                if vis is not None:
                    sp = jnp.where(vis, sp, 0.0)
                hi = sp.astype(BF16)
                lo = (sp - hi.astype(F32)).astype(BF16)
                tl[b].append(_dot(jnp.concatenate([hi, lo], axis=1), su))
        carry = list(carry)
        for b, (_, v_blk, vis) in enumerate(blocks):
            for p in pairs:
                c, acc = carry[p]
                w = jnp.exp2(lb[b][p] + tl[b][p][:, :KEY_BLOCK] + c)
                if vis is not None:
                    w = jnp.where(vis, w, 0.0)
                wb = w.astype(BF16)
                if transposed:
                    pv = lax.dot_general(wb, v_blk[sl[p], :], nt_dims, preferred_element_type=F32)
                else:
                    pv = _dot(wb, v_blk[:, sl[p]])
                carry[p] = (c + tl[b][p][:, KEY_BLOCK:], acc + pv)
        return tuple(carry)

    def new_block(j, vis):
        if new_keys_t:
            return kn_ref[0, j], vn_ref[0, j], vis
        start = pl.multiple_of(j * KEY_BLOCK, KEY_BLOCK)
        return kn_ref[0, pl.ds(start, KEY_BLOCK), :], vn_ref[0, pl.ds(start, KEY_BLOCK), :], vis

    def new_step(js, carry, vis=None):
        return step([new_block(j, vis) for j in js], carry, new_keys_t)

    zeros = jnp.zeros((2 * tq, LANES), F32)
    carry = tuple((zeros, zeros) for _ in range(HEAD_PAIRS))

    n_diag = max(tq // KEY_BLOCK, 1)
    if single_q_block:
        first_diag = 0
    else:
        first_diag = i * (tq // KEY_BLOCK) if tq >= KEY_BLOCK else (i * tq) // KEY_BLOCK

    def lead(extra):
        blocks = [new_block(first_diag + d, (first_diag + d) * KEY_BLOCK + col < q_pos)
                  for d in reversed(range(n_diag))]
        blocks += [new_block(first_diag - 1 - e, None) for e in range(extra)]
        return step(blocks, carry, new_keys_t)

    if single_q_block:
        carry, n_rest = lead(0), 0
    else:
        has_extra = first_diag >= LEAD_EXTRA
        carry = lax.cond(has_extra, lambda: lead(LEAD_EXTRA), lambda: lead(0))
        n_rest = first_diag - jnp.where(has_extra, LEAD_EXTRA, 0)

    def alive(carry):
        c_max = functools.reduce(jnp.maximum, [carry[p][0] for p in pairs])
        return (jnp.max(c_max) > DEAD_LOG2_WEIGHT).astype(jnp.int32)

    def visit(n_blocks, body, live, carry):
        def cond(state):
            return jnp.logical_and(state[0] < n_blocks, state[1] > 0)

        def loop_body(state):
            new_carry = body(state[0], state[2])
            return state[0] + 1, alive(new_carry), new_carry

        return lax.while_loop(cond, loop_body, (jnp.int32(0), live, carry))

    live = alive(carry)
    if not single_q_block:
        _, live, carry = visit(
            lax.shift_right_logical(n_rest, 1),
            lambda jj, carry: new_step([n_rest - 1 - 2 * jj, n_rest - 2 - 2 * jj], carry),
            live, carry)
        _, live, carry = visit(jnp.bitwise_and(n_rest, 1), lambda jj, carry: new_step([0], carry),
                               live, carry)

    partial = hist_valid % KEY_BLOCK

    if hist_dma:
        assert not partial
        hbuf = scratch[0]

        def hist_body(jj, carry):
            for copy in hist_copies(jj):
                copy.wait()

            @pl.when(jj + HIST_SLOTS - 1 < n_hist_blocks)
            def _():
                for copy in hist_copies(jj + HIST_SLOTS - 1):
                    copy.start()

            slot = lax.rem(jj, HIST_SLOTS)
            return step([(hbuf[slot, 0].astype(BF16), hbuf[slot, 1].astype(BF16), None)], carry, True)

        visited, _, carry = visit(n_hist_blocks, hist_body, live, carry)

        for ahead in range(HIST_SLOTS - 1):
            @pl.when(visited + ahead < n_hist_blocks)
            def _():
                for copy in hist_copies(visited + ahead):
                    copy.wait()
    else:
        assert n_hist_blocks == 1
        vis = col < partial if partial else None
        _, _, carry = visit(
            1, lambda jj, carry: step([(kh_ref[0].astype(BF16), vh_ref[0].astype(BF16), vis)],
                                      carry, True),
            live, carry)

    for p in range(HEAD_PAIRS):
        acc = carry[p][1]
        o_ref[0, :, p * LANES:(p + 1) * LANES] = (
            jnp.where(even, acc[:tq], acc[tq:]).astype(o_ref.dtype))


def _cumsum_matrix():
    r = lax.broadcasted_iota(jnp.int32, (2 * KEY_BLOCK, KEY_BLOCK + LANES), 0) % KEY_BLOCK
    c = lax.broadcasted_iota(jnp.int32, (2 * KEY_BLOCK, KEY_BLOCK + LANES), 1)
    return jnp.where((c >= KEY_BLOCK) | (r > c), -1.0, 0.0).astype(BF16)


def _attention(q, k_new, v_new, k_hist, v_hist, *, new_keys_t, hist_valid, hist_dma, tq):
    b, t, _ = q.shape
    assert t % tq == 0
    if new_keys_t:
        assert k_new.shape[1] * KEY_BLOCK == t
        new_spec = pl.BlockSpec((1,) + k_new.shape[1:], lambda bi, i: (bi, 0, 0, 0))
    else:
        assert k_new.shape[1] % KEY_BLOCK == 0 and k_new.shape[1] >= t
        new_spec = pl.BlockSpec((1,) + k_new.shape[1:], lambda bi, i: (bi, 0, 0))
    hl = k_hist.shape[2]
    assert hl % KEY_BLOCK == 0 and hist_valid <= hl
    if hist_dma:
        hist_spec = pl.BlockSpec(memory_space=pl.ANY)
        scratch = [pltpu.VMEM((HIST_SLOTS, 2, ATTN_W, KEY_BLOCK), k_hist.dtype),
                   pltpu.SemaphoreType.DMA((HIST_SLOTS, 2))]
    else:
        hist_spec = pl.BlockSpec((1, ATTN_W, KEY_BLOCK), lambda bi, i: (0, 0, 0))
        scratch = []
    kern = functools.partial(_attn_kernel, tq=tq, single_q_block=(t == tq), new_keys_t=new_keys_t,
                             n_hist_blocks=hl // KEY_BLOCK, hist_valid=hist_valid,
                             hist_dma=hist_dma)
    return pl.pallas_call(
        kern,
        grid=(b, t // tq),
        in_specs=[pl.BlockSpec((1, tq, ATTN_W), lambda bi, i: (bi, i, 0)),
                  new_spec, new_spec, hist_spec, hist_spec, _whole()],
        out_specs=pl.BlockSpec((1, tq, ATTN_W), lambda bi, i: (bi, i, 0)),
        out_shape=jax.ShapeDtypeStruct((b, t, ATTN_W), BF16),
        scratch_shapes=scratch,
        name="attn",
        compiler_params=pltpu.CompilerParams(
            dimension_semantics=("arbitrary", "arbitrary"),
            vmem_limit_bytes=VMEM_LIMIT_BYTES),
    )(q, k_new, v_new, k_hist, v_hist, _cumsum_matrix())


def _post_kernel(*refs, first_tiles):
    (x1, pooled1, attn1, x2, pooled2, attn2), weights, (y1, y2) = refs[:6], refs[6:-2], refs[-2:]
    step = pl.program_id(0)

    @pl.when(step < first_tiles)
    def _():
        _post_tile(x1, pooled1, attn1, *weights, y1)

    @pl.when(step >= first_tiles)
    def _():
        _post_tile(x2, pooled2, attn2, *weights, y2)


def _post_tile(x_ref, pooled_ref, attn_ref, lng, lnb, wg, bg, wpg, pscale, wbp, wba, wo, bo,
               l1g, l1b, w1, b1, w2, b2, l2g, l2b, y_ref):
    subs = [slice(r, r + POST_SUB) for r in range(0, x_ref.shape[0], POST_SUB)]
    groups = range(len(POOL_WINDOWS))
    h = [_layer_norm(x_ref[rows, :], lng[...], lnb[...]) for rows in subs]
    gates = [_dot(hn.astype(BF16), wg[...]) + bg[:, UQKV_W:] for hn in h]
    pool_out = []
    for rows in subs:
        pooled = pooled_ref[rows, :]
        pool_out.append(jnp.concatenate(
            [_dot(pooled[:, g * POOL_GC:(g + 1) * POOL_GC], wpg[g]) for g in groups],
            axis=-1) * pscale[...])
    branch_pool = [_dot(po.astype(BF16), wbp[...]) for po in pool_out]
    branch_attn = [_dot(attn_ref[rows, :], wba[...]) for rows in subs]
    mix = [jax.nn.sigmoid(g[:, :D_MODEL]) * bp + jax.nn.sigmoid(g[:, D_MODEL:]) * ba
           for g, bp, ba in zip(gates, branch_pool, branch_attn)]
    res = [_dot(m.astype(BF16), wo[...]) + bo[...] for m in mix]
    h1 = [_layer_norm(ALPHA * hn + r, l1g[...], l1b[...]) for hn, r in zip(h, res)]
    hid = [jnp.square(jnp.maximum(_dot(hn.astype(BF16), w1[...]) + b1[...], 0.0)) for hn in h1]
    f = [_dot(hd.astype(BF16), w2[...]) + b2[...] for hd in hid]
    for rows, hn, fn in zip(subs, h1, f):
        y_ref[rows, :] = _layer_norm(ALPHA * hn + fn, l2g[...], l2b[...])


def _post(first, second, weights, tm):
    tm1, tm2 = tm
    n1, n2 = first[0].shape[0] // tm1, second[0].shape[0] // tm2
    assert first[0].shape[0] == n1 * tm1 and second[0].shape[0] == n2 * tm2
    assert tm1 % POST_SUB == 0 and tm2 % POST_SUB == 0
    spec1 = lambda w: pl.BlockSpec((tm1, w), lambda i: (jnp.minimum(i, n1 - 1), 0))
    spec2 = lambda w: pl.BlockSpec((tm2, w), lambda i: (jnp.maximum(i - n1, 0), 0))
    widths = (D_MODEL, POOL_W, ATTN_W)
    return pl.pallas_call(
        functools.partial(_post_kernel, first_tiles=n1),
        grid=(n1 + n2,),
        in_specs=[spec1(w) for w in widths] + [spec2(w) for w in widths]
                 + [_whole() for _ in weights],
        out_specs=(spec1(D_MODEL), spec2(D_MODEL)),
        out_shape=(jax.ShapeDtypeStruct((n1 * tm1, D_MODEL), F32),
                   jax.ShapeDtypeStruct((n2 * tm2, D_MODEL), F32)),
        name="post",
        compiler_params=pltpu.CompilerParams(
            dimension_semantics=("arbitrary",), vmem_limit_bytes=VMEM_LIMIT_BYTES),
    )(*first, *second, *weights)


def _row(v):
    return v.reshape(1, -1).astype(F32)


def _pad_rows(a, rows):
    return jnp.pad(a, ((0, 0), (0, rows - a.shape[1]), (0, 0)))


def _last_rows(halo, u):
    if u.shape[1] >= POOL_STATE:
        return u[None, :, -POOL_STATE:]
    return jnp.concatenate([halo, u], axis=1)[None, :, -POOL_STATE:]


def kernel(x_prompt, x_sample, cache_k, cache_v, state_pool, meta, ln_in_g, ln_in_b, w_in, b_in,
           w_pool_grp, pool_scale, w_br_pool, w_br_attn, w_out, b_out, ln1_g, ln1_b,
           w_mlp1, b_mlp1, w_mlp2, b_mlp2, ln2_g, ln2_b):
    assert w_in.shape[0] == DEPTH
    bp, seq, _ = x_prompt.shape
    bs, dec_seq, _ = x_sample.shape
    past = cache_k.shape[2]

    lng, lnb = _row(ln_in_g), _row(ln_in_b)
    w_in_b = w_in[0].astype(BF16)
    b_in_row = _row(b_in[0])
    post_weights = (
        lng, lnb, w_in_b[:, UQKV_W:], b_in_row, w_pool_grp[0].astype(BF16), _row(pool_scale[0]),
        w_br_pool[0].astype(BF16), w_br_attn[0].astype(BF16), w_out[0].astype(BF16), _row(b_out[0]),
        _row(ln1_g[0]), _row(ln1_b[0]), w_mlp1[0].astype(BF16), _row(b_mlp1[0]),
        w_mlp2[0].astype(BF16), _row(b_mlp2[0]), _row(ln2_g[0]), _row(ln2_b[0]))

    inproj = functools.partial(_inproj, ln_g=lng, ln_b=lnb, w_in_b=w_in_b, b_in_row=b_in_row)
    xp = x_prompt.reshape(bp * seq, D_MODEL)
    xs = x_sample.reshape(bs * dec_seq, D_MODEL)
    u_m, _, _, kf_m, vf_m, kb_m, vb_m = inproj(
        meta.astype(F32), jnp.zeros((1, POOL_HALO, POOL_W), F32), tm=N_META)
    meta_t = lambda rows: jnp.pad(rows.T, ((0, 0), (0, KEY_BLOCK - N_META)))
    pooled_p, q_p, u_tail_p, kt_p, vt_p, ktb_p, vtb_p = _inproj_t(
        xp, lng, lnb, w_in_b, b_in_row,
        w_in_b[:, UQ_W:UQKV_W].T, b_in[0, UQ_W:UQKV_W].reshape(-1, 1).astype(F32),
        u_m, meta_t(kf_m), meta_t(vf_m), streams=bp, tm=512)
    halo_s = jnp.pad(state_pool[0], ((0, 0), (POOL_HALO - POOL_STATE, 0), (0, 0)))
    u_s, pooled_s, q_s, kf_s, vf_s, kb_s, vb_s = inproj(xs, halo_s, tm=512)

    attn_p = _attention(
        q_p.reshape(bp, seq, ATTN_W), ktb_p, vtb_p, meta_t(kb_m)[None], meta_t(vb_m)[None],
        new_keys_t=True, hist_valid=N_META, hist_dma=False, tq=128)

    u_s3 = u_s.reshape(bs, dec_seq, POOL_W)
    tn_s = -(-dec_seq // KEY_BLOCK) * KEY_BLOCK
    keys_on_lanes = lambda c: c.transpose(0, 2, 3, 1).reshape(bs, ATTN_W, past)
    attn_s = _attention(
        q_s.reshape(bs, dec_seq, ATTN_W),
        _pad_rows(kb_s.reshape(bs, dec_seq, ATTN_W), tn_s),
        _pad_rows(vb_s.reshape(bs, dec_seq, ATTN_W), tn_s),
        keys_on_lanes(cache_k[0]), keys_on_lanes(cache_v[0]),
        new_keys_t=False, hist_valid=past, hist_dma=True, tq=dec_seq)

    y_p, y_s = _post(
        (xp, pooled_p, attn_p.reshape(bp * seq, ATTN_W)),
        (xs, pooled_s, attn_s.reshape(bs * dec_seq, ATTN_W)),
        post_weights, tm=(512, 256))

    def seq_major(kv_t):
        full = kv_t.reshape(DEPTH, bp, SB_HEADS, SB_HEAD_DIM, N_META + seq)
        return full.transpose(0, 1, 4, 2, 3)

    assert seq >= POOL_STATE
    heads_s = (DEPTH, bs, dec_seq, SB_HEADS, SB_HEAD_DIM)
    return (
        y_p.reshape(bp, seq, D_MODEL),
        y_s.reshape(bs, dec_seq, D_MODEL),
        seq_major(kt_p),
        seq_major(vt_p),
        u_tail_p[None, :, POOL_HALO - POOL_STATE:],
        kf_s.reshape(heads_s),
        vf_s.reshape(heads_s),
        _last_rows(halo_s, u_s3),
    )
```

```python
import functools

import jax
import jax.numpy as jnp
from jax import lax
from jax.experimental import pallas as pl
from jax.experimental.pallas import tpu as pltpu

D_MODEL = 1024
N_META = 16
POOL_W = D_MODEL // 2
POOL_WINDOWS = (2, 4, 8, 16)
POOL_GC = POOL_W // len(POOL_WINDOWS)
POOL_STATE = max(POOL_WINDOWS) - 1
POOL_HALO = POOL_STATE + 1
SB_HEADS = 8
SB_HEAD_DIM = 64
ATTN_W = SB_HEADS * SB_HEAD_DIM
D_FF = 4 * D_MODEL
UQ_W = POOL_W + ATTN_W
UQKV_W = POOL_W + 3 * ATTN_W
LN_EPS = 1e-5
DEPTH = 1
ALPHA = (2.0 * DEPTH) ** 0.25
SB_SCALE = SB_HEAD_DIM ** -0.5
LOG2_E = 1.4426950408889634
Q_SCALE = SB_SCALE * LOG2_E

LANES = 128
KEY_BLOCK = LANES
HEAD_PAIRS = ATTN_W // LANES
HIST_LEAD = 2
HIST_SLOTS = HIST_LEAD + 1
LEAD_EXTRA = 2
POST_SUB = 256
INPROJ_SUB = 256
VMEM_LIMIT_BYTES = 56 * 1024 * 1024
DEAD_LOG2_WEIGHT = -160.0

F32 = jnp.float32
BF16 = jnp.bfloat16


def _layer_norm(x, g, b):
    mu = jnp.mean(x, axis=-1, keepdims=True)
    xc = x - mu
    var = jnp.mean(xc * xc, axis=-1, keepdims=True)
    return xc * lax.rsqrt(var + LN_EPS) * g + b


def _dot(a, b):
    return jnp.dot(a, b, preferred_element_type=F32)


def _whole():
    return pl.BlockSpec(memory_space=pltpu.VMEM)


def _pooled(halo, u):
    full = jnp.concatenate([halo, u], axis=0)
    outs = []
    for g, w in enumerate(POOL_WINDOWS):
        x = full[:, g * POOL_GC:(g + 1) * POOL_GC]
        s = x
        span = 1
        while span < w:
            s = s + pltpu.roll(s, span, axis=0)
            span *= 2
        outs.append(s[POOL_HALO:] * (1.0 / w) - x[POOL_HALO:])
    return jnp.concatenate(outs, axis=-1)


def _inproj_kernel(x_ref, halo_ref, g_ref, b_ref, w_ref, bias_ref,
                   u_ref, pooled_ref, q_ref, kf_ref, vf_ref, kb_ref, vb_ref):
    h = _layer_norm(x_ref[...], g_ref[...], b_ref[...])
    z = _dot(h.astype(BF16), w_ref[:, :UQKV_W]) + bias_ref[:, :UQKV_W]
    u = z[:, :POOL_W]
    u_ref[...] = u
    t = x_ref.shape[0] // halo_ref.shape[0]
    for s in range(halo_ref.shape[0]):
        rows = slice(s * t, (s + 1) * t)
        pooled_ref[rows, :] = _pooled(halo_ref[s], u[rows]).astype(BF16)
    q_ref[...] = (z[:, POOL_W:POOL_W + ATTN_W] * Q_SCALE).astype(BF16)
    k = z[:, POOL_W + ATTN_W:POOL_W + 2 * ATTN_W]
    v = z[:, POOL_W + 2 * ATTN_W:]
    kf_ref[...] = k
    vf_ref[...] = v
    kb_ref[...] = k.astype(BF16)
    vb_ref[...] = v.astype(BF16)


def _inproj(x, halo, ln_g, ln_b, w_in_b, b_in_row, tm):
    rows = x.shape[0]
    t = rows // halo.shape[0]
    assert rows % tm == 0 and tm % t == 0 and rows == t * halo.shape[0]
    row_spec = lambda w: pl.BlockSpec((tm, w), lambda i: (i, 0))
    out_shape = (
        jax.ShapeDtypeStruct((rows, POOL_W), F32),
        jax.ShapeDtypeStruct((rows, POOL_W), BF16),
        jax.ShapeDtypeStruct((rows, ATTN_W), BF16),
        jax.ShapeDtypeStruct((rows, ATTN_W), F32),
        jax.ShapeDtypeStruct((rows, ATTN_W), F32),
        jax.ShapeDtypeStruct((rows, ATTN_W), BF16),
        jax.ShapeDtypeStruct((rows, ATTN_W), BF16),
    )
    return pl.pallas_call(
        _inproj_kernel,
        grid=(rows // tm,),
        in_specs=[row_spec(D_MODEL),
                  pl.BlockSpec((tm // t, POOL_HALO, POOL_W), lambda i: (i, 0, 0)),
                  _whole(), _whole(), _whole(), _whole()],
        out_specs=tuple(row_spec(s.shape[1]) for s in out_shape),
        out_shape=out_shape,
        name="inproj",
        compiler_params=pltpu.CompilerParams(
            dimension_semantics=("arbitrary",), vmem_limit_bytes=VMEM_LIMIT_BYTES),
    )(x, halo, ln_g, ln_b, w_in_b, b_in_row)


def _inproj_t_kernel(x_ref, g_ref, b_ref, w_uq_ref, b_uq_ref, w_kvt_ref, b_kvt_ref,
                     u_meta_ref, kt_meta_ref, vt_meta_ref,
                     pooled_ref, q_ref, u_tail_ref, kt_ref, vt_ref, ktb_ref, vtb_ref,
                     u_halo, kt_carry, vt_carry):
    j = pl.program_id(1)
    tm = x_ref.shape[0]
    lane = lax.broadcasted_iota(jnp.int32, (ATTN_W, LANES), 1)

    @pl.when(j == 0)
    def _():
        u_halo[...] = u_meta_ref[...]
        kt_carry[...] = kt_meta_ref[...]
        vt_carry[...] = vt_meta_ref[...]

    def shifted(carry, cur):
        rolled = pltpu.roll(cur, N_META, axis=1)
        out = jnp.concatenate(
            [jnp.where(lane < N_META, carry, rolled[:, :LANES]), rolled[:, LANES:]], axis=1)
        return out, rolled[:, :LANES]

    @pl.when(j < pl.num_programs(1) - 1)
    def _():
        subs = [slice(r, r + INPROJ_SUB) for r in range(0, tm, INPROJ_SUB)]
        hb = [_layer_norm(x_ref[rows, :], g_ref[...], b_ref[...]).astype(BF16) for rows in subs]
        z, zt = [], []
        for h in hb:
            z.append(_dot(h, w_uq_ref[:, :UQ_W]))
            zt.append(lax.dot_general(w_kvt_ref[...], h, (((1,), (1,)), ((), ())),
                                      preferred_element_type=F32))
        halo, kt_c, vt_c = u_halo[...], kt_carry[...], vt_carry[...]
        for n, rows in enumerate(subs):
            zn = z[n] + b_uq_ref[:, :UQ_W]
            u = zn[:, :POOL_W]
            q_ref[rows, :] = (zn[:, POOL_W:] * Q_SCALE).astype(BF16)
            pooled_ref[rows, :] = _pooled(halo, u).astype(BF16)
            halo = u[INPROJ_SUB - POOL_HALO:]
            ztn = zt[n] + b_kvt_ref[...]
            kt, vt = ztn[:ATTN_W], ztn[ATTN_W:]
            for c in range(INPROJ_SUB // KEY_BLOCK):
                cols = slice(c * KEY_BLOCK, (c + 1) * KEY_BLOCK)
                blk = rows.start // KEY_BLOCK + c
                ktb_ref[0, blk] = kt[:, cols].astype(BF16)
                vtb_ref[0, blk] = vt[:, cols].astype(BF16)
            kt_ref[0, :, rows], kt_c = shifted(kt_c, kt)
            vt_ref[0, :, rows], vt_c = shifted(vt_c, vt)
        u_halo[...] = halo
        u_tail_ref[0] = halo
        kt_carry[...] = kt_c
        vt_carry[...] = vt_c

    @pl.when(j == pl.num_programs(1) - 1)
    def _():
        kt_ref[0] = jnp.zeros(kt_ref.shape[1:], F32)
        vt_ref[0] = jnp.zeros(vt_ref.shape[1:], F32)
        kt_ref[0, :, :LANES] = kt_carry[...]
        vt_ref[0, :, :LANES] = vt_carry[...]


def _inproj_t(x, ln_g, ln_b, w_uq, b_uq, w_kvt, b_kvt, u_meta, kt_meta, vt_meta, streams, tm):
    rows = x.shape[0]
    t = rows // streams
    assert rows == streams * t and t % tm == 0 and POOL_HALO == N_META
    assert tm % INPROJ_SUB == 0 and INPROJ_SUB % KEY_BLOCK == 0
    per = t // tm
    kb = tm // KEY_BLOCK
    last = per - 1
    row_spec = lambda w: pl.BlockSpec((tm, w), lambda s, j: (s * per + jnp.minimum(j, last), 0))
    t_spec = pl.BlockSpec((1, ATTN_W, tm), lambda s, j: (s, 0, j))
    tb_spec = pl.BlockSpec((1, kb, ATTN_W, KEY_BLOCK),
                           lambda s, j: (s, jnp.minimum(j, last), 0, 0))
    out_shape = (
        jax.ShapeDtypeStruct((rows, POOL_W), BF16),
        jax.ShapeDtypeStruct((rows, ATTN_W), BF16),
        jax.ShapeDtypeStruct((streams, POOL_HALO, POOL_W), F32),
        jax.ShapeDtypeStruct((streams, ATTN_W, N_META + t), F32),
        jax.ShapeDtypeStruct((streams, ATTN_W, N_META + t), F32),
        jax.ShapeDtypeStruct((streams, t // KEY_BLOCK, ATTN_W, KEY_BLOCK), BF16),
        jax.ShapeDtypeStruct((streams, t // KEY_BLOCK, ATTN_W, KEY_BLOCK), BF16),
    )
    return pl.pallas_call(
        _inproj_t_kernel,
        grid=(streams, per + 1),
        in_specs=[row_spec(D_MODEL)] + [_whole()] * 9,
        out_specs=(row_spec(POOL_W), row_spec(ATTN_W),
                   pl.BlockSpec((1, POOL_HALO, POOL_W), lambda s, j: (s, 0, 0)),
                   t_spec, t_spec, tb_spec, tb_spec),
        out_shape=out_shape,
        scratch_shapes=[pltpu.VMEM((POOL_HALO, POOL_W), F32),
                        pltpu.VMEM((ATTN_W, LANES), F32), pltpu.VMEM((ATTN_W, LANES), F32)],
        name="inproj_t",
        compiler_params=pltpu.CompilerParams(
            dimension_semantics=("arbitrary", "arbitrary"), vmem_limit_bytes=VMEM_LIMIT_BYTES),
    )(x, ln_g, ln_b, w_uq, b_uq, w_kvt, b_kvt, u_meta, kt_meta, vt_meta)


def _attn_kernel(q_ref, kn_ref, vn_ref, kh_ref, vh_ref, su_ref, o_ref, *scratch,
                 tq, single_q_block, new_keys_t, n_hist_blocks, hist_valid, hist_dma):
    stream = pl.program_id(0)
    i = pl.program_id(1)

    n_lead_hist = min(HIST_LEAD, n_hist_blocks) if hist_dma else 0

    def hist_copies(of_stream, jj, slot):
        hbuf, hsem = scratch
        start = pl.multiple_of((n_hist_blocks - 1 - jj) * KEY_BLOCK, KEY_BLOCK)
        return [pltpu.make_async_copy(src.at[of_stream, :, pl.ds(start, KEY_BLOCK)],
                                      hbuf.at[slot, which], hsem.at[slot, which])
                for which, src in enumerate((kh_ref, vh_ref))]

    def lead_hist_copies(of_stream):
        return [copy for jj in range(n_lead_hist) for copy in hist_copies(of_stream, jj, jj)]

    if hist_dma:
        assert single_q_block

        @pl.when(stream == 0)
        def _():
            for copy in lead_hist_copies(stream):
                copy.start()

    lane = lax.broadcasted_iota(jnp.int32, (tq, LANES), 1)
    even = lane < SB_HEAD_DIM
    qs = []
    for p in range(HEAD_PAIRS):
        q = q_ref[0, :, p * LANES:(p + 1) * LANES]
        zero = jnp.zeros_like(q)
        qs.append(jnp.concatenate([jnp.where(even, q, zero), jnp.where(even, zero, q)], axis=0))
    su = su_ref[...]

    row = lax.broadcasted_iota(jnp.int32, (2 * tq, KEY_BLOCK), 0)
    col = lax.broadcasted_iota(jnp.int32, (2 * tq, KEY_BLOCK), 1)
    q_pos = jnp.where(row >= tq, row - tq, row) + i * tq

    pairs = range(HEAD_PAIRS)

    nt_dims = (((1,), (1,)), ((), ()))

    def step(blocks, carry):
        sl = [slice(p * LANES, (p + 1) * LANES) for p in pairs]
        s = []
        for k_blk, _, _, transposed in blocks:
            if transposed:
                s.append([_dot(qs[p], k_blk[sl[p], :]) for p in pairs])
            else:
                s.append([lax.dot_general(qs[p], k_blk[:, sl[p]], nt_dims,
                                          preferred_element_type=F32) for p in pairs])
        tl = []
        for b, (_, _, vis, _) in enumerate(blocks):
            tl.append([])
            for p in pairs:
                z2 = s[b][p]
                sp = jnp.maximum(z2, 0.0) + jnp.log(1.0 + jnp.exp2(-jnp.abs(z2))) * LOG2_E
                if vis is not None:
                    sp = jnp.where(vis, sp, 0.0)
                hi = sp.astype(BF16)
                lo = (sp - hi.astype(F32)).astype(BF16)
                tl[b].append(_dot(jnp.concatenate([hi, lo], axis=1), su))
        carry = list(carry)
        for b, (_, v_blk, vis, transposed) in enumerate(blocks):
            for p in pairs:
                c, acc = carry[p]
                w = jnp.exp2(s[b][p] + tl[b][p][:, :KEY_BLOCK] + c)
                if vis is not None:
                    w = jnp.where(vis, w, 0.0)
                wb = w.astype(BF16)
                if transposed:
                    pv = lax.dot_general(wb, v_blk[sl[p], :], nt_dims, preferred_element_type=F32)
                else:
                    pv = _dot(wb, v_blk[:, sl[p]])
                carry[p] = (c + tl[b][p][:, KEY_BLOCK:], acc + pv)
        return tuple(carry)

    def new_block(j, vis):
        if new_keys_t:
            return kn_ref[0, j], vn_ref[0, j], vis, True
        start = pl.multiple_of(j * KEY_BLOCK, KEY_BLOCK)
        return (kn_ref[0, pl.ds(start, KEY_BLOCK), :], vn_ref[0, pl.ds(start, KEY_BLOCK), :], vis,
                False)

    def hist_block(slot):
        hbuf = scratch[0]
        return hbuf[slot, 0].astype(BF16), hbuf[slot, 1].astype(BF16), None, True

    def new_step(js, carry, vis=None):
        return step([new_block(j, vis) for j in js], carry)

    zeros = jnp.zeros((2 * tq, LANES), F32)
    carry = tuple((zeros, zeros) for _ in range(HEAD_PAIRS))

    n_diag = max(tq // KEY_BLOCK, 1)
    if single_q_block:
        first_diag = 0
    else:
        first_diag = i * (tq // KEY_BLOCK) if tq >= KEY_BLOCK else (i * tq) // KEY_BLOCK

    def lead(extra):
        blocks = [new_block(first_diag + d, (first_diag + d) * KEY_BLOCK + col < q_pos)
                  for d in reversed(range(n_diag))]
        blocks += [new_block(first_diag - 1 - e, None) for e in range(extra)]
        if single_q_block:
            blocks += [hist_block(jj) for jj in range(n_lead_hist)]
        return step(blocks, carry)

    if single_q_block:
        for copy in lead_hist_copies(stream):
            copy.wait()
        carry, n_rest = lead(0), 0
    else:
        has_extra = first_diag >= LEAD_EXTRA
        carry = lax.cond(has_extra, lambda: lead(LEAD_EXTRA), lambda: lead(0))
        n_rest = first_diag - jnp.where(has_extra, LEAD_EXTRA, 0)

    def alive(carry):
        c_max = functools.reduce(jnp.maximum, [carry[p][0] for p in pairs])
        return (jnp.max(c_max) > DEAD_LOG2_WEIGHT).astype(jnp.int32)

    def visit(n_blocks, body, live, carry):
        def cond(state):
            return jnp.logical_and(state[0] < n_blocks, state[1] > 0)

        def loop_body(state):
            new_carry = body(state[0], state[2])
            return state[0] + 1, alive(new_carry), new_carry

        return lax.while_loop(cond, loop_body, (jnp.int32(0), live, carry))

    live = alive(carry)
    if not single_q_block:
        _, live, carry = visit(
            lax.shift_right_logical(n_rest, 1),
            lambda jj, carry: new_step([n_rest - 1 - 2 * jj, n_rest - 2 - 2 * jj], carry),
            live, carry)
        _, live, carry = visit(jnp.bitwise_and(n_rest, 1), lambda jj, carry: new_step([0], carry),
                               live, carry)

    partial = hist_valid % KEY_BLOCK

    if hist_dma:
        assert not partial

        def hist_body(jj, carry):
            copies = hist_copies(stream, n_lead_hist + jj, n_lead_hist)
            for copy in copies:
                copy.start()
            for copy in copies:
                copy.wait()
            return step([hist_block(n_lead_hist)], carry)

        _, _, carry = visit(n_hist_blocks - n_lead_hist, hist_body, live, carry)

        @pl.when(stream + 1 < pl.num_programs(0))
        def _():
            for copy in lead_hist_copies(stream + 1):
                copy.start()
    else:
        assert n_hist_blocks == 1
        vis = col < partial if partial else None
        _, _, carry = visit(
            1, lambda jj, carry: step([(kh_ref[0].astype(BF16), vh_ref[0].astype(BF16), vis, True)],
                                      carry),
            live, carry)

    for p in range(HEAD_PAIRS):
        acc = carry[p][1]
        o_ref[0, :, p * LANES:(p + 1) * LANES] = (
            jnp.where(even, acc[:tq], acc[tq:]).astype(o_ref.dtype))


def _cumsum_matrix():
    r = lax.broadcasted_iota(jnp.int32, (2 * KEY_BLOCK, KEY_BLOCK + LANES), 0) % KEY_BLOCK
    c = lax.broadcasted_iota(jnp.int32, (2 * KEY_BLOCK, KEY_BLOCK + LANES), 1)
    return jnp.where((c >= KEY_BLOCK) | (r >= c), -1.0, 0.0).astype(BF16)


def _attention(q, k_new, v_new, k_hist, v_hist, *, new_keys_t, hist_valid, hist_dma, tq):
    b, t, _ = q.shape
    assert t % tq == 0
    if new_keys_t:
        assert k_new.shape[1] * KEY_BLOCK == t
        new_spec = pl.BlockSpec((1,) + k_new.shape[1:], lambda bi, i: (bi, 0, 0, 0))
    else:
        assert k_new.shape[1] % KEY_BLOCK == 0 and k_new.shape[1] >= t
        new_spec = pl.BlockSpec((1,) + k_new.shape[1:], lambda bi, i: (bi, 0, 0))
    hl = k_hist.shape[2]
    assert hl % KEY_BLOCK == 0 and hist_valid <= hl
    if hist_dma:
        hist_spec = pl.BlockSpec(memory_space=pl.ANY)
        scratch = [pltpu.VMEM((HIST_SLOTS, 2, ATTN_W, KEY_BLOCK), k_hist.dtype),
                   pltpu.SemaphoreType.DMA((HIST_SLOTS, 2))]
    else:
        hist_spec = pl.BlockSpec((1, ATTN_W, KEY_BLOCK), lambda bi, i: (0, 0, 0))
        scratch = []
    kern = functools.partial(_attn_kernel, tq=tq, single_q_block=(t == tq), new_keys_t=new_keys_t,
                             n_hist_blocks=hl // KEY_BLOCK, hist_valid=hist_valid,
                             hist_dma=hist_dma)
    return pl.pallas_call(
        kern,
        grid=(b, t // tq),
        in_specs=[pl.BlockSpec((1, tq, ATTN_W), lambda bi, i: (bi, i, 0)),
                  new_spec, new_spec, hist_spec, hist_spec, _whole()],
        out_specs=pl.BlockSpec((1, tq, ATTN_W), lambda bi, i: (bi, i, 0)),
        out_shape=jax.ShapeDtypeStruct((b, t, ATTN_W), BF16),
        scratch_shapes=scratch,
        name="attn",
        compiler_params=pltpu.CompilerParams(
            dimension_semantics=("arbitrary", "arbitrary"),
            vmem_limit_bytes=VMEM_LIMIT_BYTES),
    )(q, k_new, v_new, k_hist, v_hist, _cumsum_matrix())


def _post_kernel(*refs, first_tiles):
    (x1, pooled1, attn1, x2, pooled2, attn2), weights, (y1, y2) = refs[:6], refs[6:-2], refs[-2:]
    step = pl.program_id(0)

    @pl.when(step < first_tiles)
    def _():
        _post_tile(x1, pooled1, attn1, *weights, y1)

    @pl.when(step >= first_tiles)
    def _():
        _post_tile(x2, pooled2, attn2, *weights, y2)


def _post_tile(x_ref, pooled_ref, attn_ref, lng, lnb, wg, bg, wpg, pscale, wbp, wba, wo, bo,
               l1g, l1b, w1, b1, w2, b2, l2g, l2b, y_ref):
    subs = [slice(r, r + POST_SUB) for r in range(0, x_ref.shape[0], POST_SUB)]
    groups = range(len(POOL_WINDOWS))
    h = [_layer_norm(x_ref[rows, :], lng[...], lnb[...]) for rows in subs]
    gates = [_dot(hn.astype(BF16), wg[...]) + bg[:, UQKV_W:] for hn in h]
    pool_out = []
    for rows in subs:
        pooled = pooled_ref[rows, :]
        pool_out.append(jnp.concatenate(
            [_dot(pooled[:, g * POOL_GC:(g + 1) * POOL_GC], wpg[g]) for g in groups],
            axis=-1) * pscale[...])
    branch_pool = [_dot(po.astype(BF16), wbp[...]) for po in pool_out]
    branch_attn = [_dot(attn_ref[rows, :], wba[...]) for rows in subs]
    mix = [jax.nn.sigmoid(g[:, :D_MODEL]) * bp + jax.nn.sigmoid(g[:, D_MODEL:]) * ba
           for g, bp, ba in zip(gates, branch_pool, branch_attn)]
    res = [_dot(m.astype(BF16), wo[...]) + bo[...] for m in mix]
    h1 = [_layer_norm(ALPHA * hn + r, l1g[...], l1b[...]) for hn, r in zip(h, res)]
    hid = [jnp.square(jnp.maximum(_dot(hn.astype(BF16), w1[...]) + b1[...], 0.0)) for hn in h1]
    f = [_dot(hd.astype(BF16), w2[...]) + b2[...] for hd in hid]
    for rows, hn, fn in zip(subs, h1, f):
        y_ref[rows, :] = _layer_norm(ALPHA * hn + fn, l2g[...], l2b[...])


def _post(first, second, weights, tm):
    tm1, tm2 = tm
    n1, n2 = first[0].shape[0] // tm1, second[0].shape[0] // tm2
    assert first[0].shape[0] == n1 * tm1 and second[0].shape[0] == n2 * tm2
    assert tm1 % POST_SUB == 0 and tm2 % POST_SUB == 0
    spec1 = lambda w: pl.BlockSpec((tm1, w), lambda i: (jnp.minimum(i, n1 - 1), 0))
    spec2 = lambda w: pl.BlockSpec((tm2, w), lambda i: (jnp.maximum(i - n1, 0), 0))
    widths = (D_MODEL, POOL_W, ATTN_W)
    return pl.pallas_call(
        functools.partial(_post_kernel, first_tiles=n1),
        grid=(n1 + n2,),
        in_specs=[spec1(w) for w in widths] + [spec2(w) for w in widths]
                 + [_whole() for _ in weights],
        out_specs=(spec1(D_MODEL), spec2(D_MODEL)),
        out_shape=(jax.ShapeDtypeStruct((n1 * tm1, D_MODEL), F32),
                   jax.ShapeDtypeStruct((n2 * tm2, D_MODEL), F32)),
        name="post",
        compiler_params=pltpu.CompilerParams(
            dimension_semantics=("arbitrary",), vmem_limit_bytes=VMEM_LIMIT_BYTES),
    )(*first, *second, *weights)


def _row(v):
    return v.reshape(1, -1).astype(F32)


def _pad_rows(a, rows):
    return jnp.pad(a, ((0, 0), (0, rows - a.shape[1]), (0, 0)))


def _last_rows(halo, u):
    if u.shape[1] >= POOL_STATE:
        return u[None, :, -POOL_STATE:]
    return jnp.concatenate([halo, u], axis=1)[None, :, -POOL_STATE:]


def kernel(x_prompt, x_sample, cache_k, cache_v, state_pool, meta, ln_in_g, ln_in_b, w_in, b_in,
           w_pool_grp, pool_scale, w_br_pool, w_br_attn, w_out, b_out, ln1_g, ln1_b,
           w_mlp1, b_mlp1, w_mlp2, b_mlp2, ln2_g, ln2_b):
    assert w_in.shape[0] == DEPTH
    bp, seq, _ = x_prompt.shape
    bs, dec_seq, _ = x_sample.shape
    past = cache_k.shape[2]

    lng, lnb = _row(ln_in_g), _row(ln_in_b)
    w_in_b = w_in[0].astype(BF16)
    b_in_row = _row(b_in[0])
    post_weights = (
        lng, lnb, w_in_b[:, UQKV_W:], b_in_row, w_pool_grp[0].astype(BF16), _row(pool_scale[0]),
        w_br_pool[0].astype(BF16), w_br_attn[0].astype(BF16), w_out[0].astype(BF16), _row(b_out[0]),
        _row(ln1_g[0]), _row(ln1_b[0]), w_mlp1[0].astype(BF16), _row(b_mlp1[0]),
        w_mlp2[0].astype(BF16), _row(b_mlp2[0]), _row(ln2_g[0]), _row(ln2_b[0]))

    inproj = functools.partial(_inproj, ln_g=lng, ln_b=lnb, w_in_b=w_in_b, b_in_row=b_in_row)
    xp = x_prompt.reshape(bp * seq, D_MODEL)
    xs = x_sample.reshape(bs * dec_seq, D_MODEL)
    u_m, _, _, kf_m, vf_m, kb_m, vb_m = inproj(
        meta.astype(F32), jnp.zeros((1, POOL_HALO, POOL_W), F32), tm=N_META)
    meta_t = lambda rows: jnp.pad(rows.T, ((0, 0), (0, KEY_BLOCK - N_META)))
    pooled_p, q_p, u_tail_p, kt_p, vt_p, ktb_p, vtb_p = _inproj_t(
        xp, lng, lnb, w_in_b, b_in_row,
        w_in_b[:, UQ_W:UQKV_W].T, b_in[0, UQ_W:UQKV_W].reshape(-1, 1).astype(F32),
        u_m, meta_t(kf_m), meta_t(vf_m), streams=bp, tm=1024)
    halo_s = jnp.pad(state_pool[0], ((0, 0), (POOL_HALO - POOL_STATE, 0), (0, 0)))
    u_s, pooled_s, q_s, kf_s, vf_s, kb_s, vb_s = inproj(xs, halo_s, tm=512)

    attn_p = _attention(
        q_p.reshape(bp, seq, ATTN_W), ktb_p, vtb_p, meta_t(kb_m)[None], meta_t(vb_m)[None],
        new_keys_t=True, hist_valid=N_META, hist_dma=False, tq=128)

    u_s3 = u_s.reshape(bs, dec_seq, POOL_W)
    tn_s = -(-dec_seq // KEY_BLOCK) * KEY_BLOCK
    keys_on_lanes = lambda c: c.transpose(0, 2, 3, 1).reshape(bs, ATTN_W, past)
    attn_s = _attention(
        q_s.reshape(bs, dec_seq, ATTN_W),
        _pad_rows(kb_s.reshape(bs, dec_seq, ATTN_W), tn_s),
        _pad_rows(vb_s.reshape(bs, dec_seq, ATTN_W), tn_s),
        keys_on_lanes(cache_k[0]), keys_on_lanes(cache_v[0]),
        new_keys_t=False, hist_valid=past, hist_dma=True, tq=dec_seq)

    y_p, y_s = _post(
        (xp, pooled_p, attn_p.reshape(bp * seq, ATTN_W)),
        (xs, pooled_s, attn_s.reshape(bs * dec_seq, ATTN_W)),
        post_weights, tm=(512, 256))

    def seq_major(kv_t):
        full = kv_t.reshape(DEPTH, bp, SB_HEADS, SB_HEAD_DIM, N_META + seq)
        return full.transpose(0, 1, 4, 2, 3)

    assert seq >= POOL_STATE
    heads_s = (DEPTH, bs, dec_seq, SB_HEADS, SB_HEAD_DIM)
    return (
        y_p.reshape(bp, seq, D_MODEL),
        y_s.reshape(bs, dec_seq, D_MODEL),
        seq_major(kt_p),
        seq_major(vt_p),
        u_tail_p[None, :, POOL_HALO - POOL_STATE:],
        kf_s.reshape(heads_s),
        vf_s.reshape(heads_s),
        _last_rows(halo_s, u_s3),
    )
```

```python
import functools

import jax
import jax.numpy as jnp
from jax import lax
from jax.experimental import pallas as pl
from jax.experimental.pallas import tpu as pltpu

D_MODEL = 1024
N_META = 16
POOL_W = D_MODEL // 2
POOL_WINDOWS = (2, 4, 8, 16)
POOL_GC = POOL_W // len(POOL_WINDOWS)
POOL_STATE = max(POOL_WINDOWS) - 1
POOL_HALO = POOL_STATE + 1
SB_HEADS = 8
SB_HEAD_DIM = 64
ATTN_W = SB_HEADS * SB_HEAD_DIM
D_FF = 4 * D_MODEL
UQ_W = POOL_W + ATTN_W
UQKV_W = POOL_W + 3 * ATTN_W
LN_EPS = 1e-5
DEPTH = 1
ALPHA = (2.0 * DEPTH) ** 0.25
SB_SCALE = SB_HEAD_DIM ** -0.5
LOG2_E = 1.4426950408889634
Q_SCALE = SB_SCALE * LOG2_E

LANES = 128
KEY_BLOCK = LANES
HEAD_PAIRS = ATTN_W // LANES
STAGE_SKEW = 2
SKEW_MIN_ROWS = 256
HIST_LEAD = 2
HIST_SLOTS = HIST_LEAD + 1
LEAD_EXTRA = 2
POST_SUB = 256
INPROJ_SUB = 256
VMEM_LIMIT_BYTES = 56 * 1024 * 1024
DEAD_LOG2_WEIGHT = -160.0

F32 = jnp.float32
BF16 = jnp.bfloat16


def _layer_norm(x, g, b):
    mu = jnp.mean(x, axis=-1, keepdims=True)
    xc = x - mu
    var = jnp.mean(xc * xc, axis=-1, keepdims=True)
    return xc * lax.rsqrt(var + LN_EPS) * g + b


def _dot(a, b):
    return jnp.dot(a, b, preferred_element_type=F32)


def _whole():
    return pl.BlockSpec(memory_space=pltpu.VMEM)


def _pooled(halo, u):
    full = jnp.concatenate([halo, u], axis=0)
    outs = []
    for g, w in enumerate(POOL_WINDOWS):
        x = full[:, g * POOL_GC:(g + 1) * POOL_GC]
        s = x
        span = 1
        while span < w:
            s = s + pltpu.roll(s, span, axis=0)
            span *= 2
        outs.append(s[POOL_HALO:] * (1.0 / w) - x[POOL_HALO:])
    return jnp.concatenate(outs, axis=-1)


def _inproj_kernel(x_ref, halo_ref, g_ref, b_ref, w_ref, bias_ref,
                   u_ref, pooled_ref, q_ref, kf_ref, vf_ref, kb_ref, vb_ref):
    h = _layer_norm(x_ref[...], g_ref[...], b_ref[...])
    z = _dot(h.astype(BF16), w_ref[:, :UQKV_W]) + bias_ref[:, :UQKV_W]
    u = z[:, :POOL_W]
    u_ref[...] = u
    t = x_ref.shape[0] // halo_ref.shape[0]
    for s in range(halo_ref.shape[0]):
        rows = slice(s * t, (s + 1) * t)
        pooled_ref[rows, :] = _pooled(halo_ref[s], u[rows]).astype(BF16)
    q_ref[...] = (z[:, POOL_W:POOL_W + ATTN_W] * Q_SCALE).astype(BF16)
    k = z[:, POOL_W + ATTN_W:POOL_W + 2 * ATTN_W]
    v = z[:, POOL_W + 2 * ATTN_W:]
    kf_ref[...] = k
    vf_ref[...] = v
    kb_ref[...] = k.astype(BF16)
    vb_ref[...] = v.astype(BF16)


def _inproj(x, halo, ln_g, ln_b, w_in_b, b_in_row, tm):
    rows = x.shape[0]
    t = rows // halo.shape[0]
    assert rows % tm == 0 and tm % t == 0 and rows == t * halo.shape[0]
    row_spec = lambda w: pl.BlockSpec((tm, w), lambda i: (i, 0))
    out_shape = (
        jax.ShapeDtypeStruct((rows, POOL_W), F32),
        jax.ShapeDtypeStruct((rows, POOL_W), BF16),
        jax.ShapeDtypeStruct((rows, ATTN_W), BF16),
        jax.ShapeDtypeStruct((rows, ATTN_W), F32),
        jax.ShapeDtypeStruct((rows, ATTN_W), F32),
        jax.ShapeDtypeStruct((rows, ATTN_W), BF16),
        jax.ShapeDtypeStruct((rows, ATTN_W), BF16),
    )
    return pl.pallas_call(
        _inproj_kernel,
        grid=(rows // tm,),
        in_specs=[row_spec(D_MODEL),
                  pl.BlockSpec((tm // t, POOL_HALO, POOL_W), lambda i: (i, 0, 0)),
                  _whole(), _whole(), _whole(), _whole()],
        out_specs=tuple(row_spec(s.shape[1]) for s in out_shape),
        out_shape=out_shape,
        name="inproj",
        compiler_params=pltpu.CompilerParams(
            dimension_semantics=("arbitrary",), vmem_limit_bytes=VMEM_LIMIT_BYTES),
    )(x, halo, ln_g, ln_b, w_in_b, b_in_row)


def _inproj_t_kernel(x_ref, g_ref, b_ref, w_uq_ref, b_uq_ref, w_kvt_ref, b_kvt_ref,
                     u_meta_ref, kt_meta_ref, vt_meta_ref,
                     pooled_ref, q_ref, u_tail_ref, kt_ref, vt_ref, ktb_ref, vtb_ref,
                     u_halo, kt_carry, vt_carry):
    j = pl.program_id(1)
    tm = x_ref.shape[0]
    t = kt_ref.shape[2] - N_META
    lane = lax.broadcasted_iota(jnp.int32, (ATTN_W, LANES), 1)

    @pl.when(j == 0)
    def _():
        u_halo[...] = u_meta_ref[...]
        kt_carry[...] = kt_meta_ref[...]
        vt_carry[...] = vt_meta_ref[...]

    def shifted(carry, cur):
        rolled = pltpu.roll(cur, N_META, axis=1)
        out = jnp.concatenate(
            [jnp.where(lane < N_META, carry, rolled[:, :LANES]), rolled[:, LANES:]], axis=1)
        return out, rolled[:, :LANES]

    subs = [slice(r, r + INPROJ_SUB) for r in range(0, tm, INPROJ_SUB)]
    hb = [_layer_norm(x_ref[rows, :], g_ref[...], b_ref[...]).astype(BF16) for rows in subs]
    z, zt = [], []
    for h in hb:
        z.append(_dot(h, w_uq_ref[:, :UQ_W]))
        zt.append(lax.dot_general(w_kvt_ref[...], h, (((1,), (1,)), ((), ())),
                                  preferred_element_type=F32))
    halo, kt_c, vt_c = u_halo[...], kt_carry[...], vt_carry[...]
    for n, rows in enumerate(subs):
        zn = z[n] + b_uq_ref[:, :UQ_W]
        u = zn[:, :POOL_W]
        q_ref[rows, :] = (zn[:, POOL_W:] * Q_SCALE).astype(BF16)
        pooled_ref[rows, :] = _pooled(halo, u).astype(BF16)
        halo = u[INPROJ_SUB - POOL_HALO:]
        ztn = zt[n] + b_kvt_ref[...]
        kt, vt = ztn[:ATTN_W], ztn[ATTN_W:]
        for c in range(INPROJ_SUB // KEY_BLOCK):
            cols = slice(c * KEY_BLOCK, (c + 1) * KEY_BLOCK)
            blk = rows.start // KEY_BLOCK + c
            ktb_ref[0, blk] = kt[:, cols].astype(BF16)
            vtb_ref[0, blk] = vt[:, cols].astype(BF16)
        out_cols = pl.ds(pl.multiple_of(j * tm + rows.start, LANES), INPROJ_SUB)
        kt_ref[0, :, out_cols], kt_c = shifted(kt_c, kt)
        vt_ref[0, :, out_cols], vt_c = shifted(vt_c, vt)
    u_halo[...] = halo
    u_tail_ref[0] = halo
    kt_carry[...] = kt_c
    vt_carry[...] = vt_c

    @pl.when(j == pl.num_programs(1) - 1)
    def _():
        kt_ref[0, :, t:] = kt_c[:, :N_META]
        vt_ref[0, :, t:] = vt_c[:, :N_META]


def _inproj_t(x, ln_g, ln_b, w_uq, b_uq, w_kvt, b_kvt, u_meta, kt_meta, vt_meta, streams, tm):
    rows = x.shape[0]
    t = rows // streams
    assert rows == streams * t and t % tm == 0 and POOL_HALO == N_META
    assert tm % INPROJ_SUB == 0 and INPROJ_SUB % KEY_BLOCK == 0
    per = t // tm
    kb = tm // KEY_BLOCK
    row_spec = lambda w: pl.BlockSpec((tm, w), lambda s, j: (s * per + j, 0))
    t_spec = pl.BlockSpec((1, ATTN_W, N_META + t), lambda s, j: (s, 0, 0))
    tb_spec = pl.BlockSpec((1, kb, ATTN_W, KEY_BLOCK), lambda s, j: (s, j, 0, 0))
    out_shape = (
        jax.ShapeDtypeStruct((rows, POOL_W), BF16),
        jax.ShapeDtypeStruct((rows, ATTN_W), BF16),
        jax.ShapeDtypeStruct((streams, POOL_HALO, POOL_W), F32),
        jax.ShapeDtypeStruct((streams, ATTN_W, N_META + t), F32),
        jax.ShapeDtypeStruct((streams, ATTN_W, N_META + t), F32),
        jax.ShapeDtypeStruct((streams, t // KEY_BLOCK, ATTN_W, KEY_BLOCK), BF16),
        jax.ShapeDtypeStruct((streams, t // KEY_BLOCK, ATTN_W, KEY_BLOCK), BF16),
    )
    return pl.pallas_call(
        _inproj_t_kernel,
        grid=(streams, per),
        in_specs=[row_spec(D_MODEL)] + [_whole()] * 9,
        out_specs=(row_spec(POOL_W), row_spec(ATTN_W),
                   pl.BlockSpec((1, POOL_HALO, POOL_W), lambda s, j: (s, 0, 0)),
                   t_spec, t_spec, tb_spec, tb_spec),
        out_shape=out_shape,
        scratch_shapes=[pltpu.VMEM((POOL_HALO, POOL_W), F32),
                        pltpu.VMEM((ATTN_W, LANES), F32), pltpu.VMEM((ATTN_W, LANES), F32)],
        name="inproj_t",
        compiler_params=pltpu.CompilerParams(
            dimension_semantics=("arbitrary", "arbitrary"), vmem_limit_bytes=VMEM_LIMIT_BYTES),
    )(x, ln_g, ln_b, w_uq, b_uq, w_kvt, b_kvt, u_meta, kt_meta, vt_meta)


def _attn_kernel(q_ref, kn_ref, vn_ref, kh_ref, vh_ref, su_ref, o_ref, *scratch,
                 tq, single_q_block, new_keys_t, n_hist_blocks, hist_valid, hist_dma):
    stream = pl.program_id(0)
    i = pl.program_id(1)

    n_lead_hist = min(HIST_LEAD, n_hist_blocks) if hist_dma else 0

    def hist_copies(of_stream, jj, slot):
        hbuf, hsem = scratch
        start = pl.multiple_of((n_hist_blocks - 1 - jj) * KEY_BLOCK, KEY_BLOCK)
        return [pltpu.make_async_copy(src.at[of_stream, :, pl.ds(start, KEY_BLOCK)],
                                      hbuf.at[slot, which], hsem.at[slot, which])
                for which, src in enumerate((kh_ref, vh_ref))]

    def lead_hist_copies(of_stream):
        return [copy for jj in range(n_lead_hist) for copy in hist_copies(of_stream, jj, jj)]

    if hist_dma:
        assert single_q_block

        @pl.when(stream == 0)
        def _():
            for copy in lead_hist_copies(stream):
                copy.start()

    lane = lax.broadcasted_iota(jnp.int32, (tq, LANES), 1)
    even = lane < SB_HEAD_DIM
    qs = []
    for p in range(HEAD_PAIRS):
        q = q_ref[0, :, p * LANES:(p + 1) * LANES]
        zero = jnp.zeros_like(q)
        qs.append(jnp.concatenate([jnp.where(even, q, zero), jnp.where(even, zero, q)], axis=0))
    su = su_ref[...]

    row = lax.broadcasted_iota(jnp.int32, (2 * tq, KEY_BLOCK), 0)
    col = lax.broadcasted_iota(jnp.int32, (2 * tq, KEY_BLOCK), 1)
    q_pos = jnp.where(row >= tq, row - tq, row) + i * tq

    pairs = range(HEAD_PAIRS)

    nt_dims = (((1,), (1,)), ((), ()))

    def step(blocks, carry):
        sl = [slice(p * LANES, (p + 1) * LANES) for p in pairs]
        chains = [(b, p) for b in range(len(blocks)) for p in pairs]
        s, tl = {}, {}
        carry = list(carry)

        def scores(b, p):
            k_blk, _, _, transposed = blocks[b]
            if transposed:
                s[b, p] = _dot(qs[p], k_blk[sl[p], :])
            else:
                s[b, p] = lax.dot_general(qs[p], k_blk[:, sl[p]], nt_dims,
                                          preferred_element_type=F32)

        def cumulate(b, p):
            vis = blocks[b][2]
            z2 = s[b, p]
            sp = jnp.maximum(z2, 0.0) + jnp.log(1.0 + jnp.exp2(-jnp.abs(z2))) * LOG2_E
            if vis is not None:
                sp = jnp.where(vis, sp, 0.0)
            hi = sp.astype(BF16)
            lo = (sp - hi.astype(F32)).astype(BF16)
            tl[b, p] = _dot(jnp.concatenate([hi, lo], axis=1), su)

        def weigh(b, p):
            _, v_blk, vis, transposed = blocks[b]
            c, acc = carry[p]
            w = jnp.exp2(s[b, p] + tl[b, p][:, :KEY_BLOCK] + c)
            if vis is not None:
                w = jnp.where(vis, w, 0.0)
            wb = w.astype(BF16)
            if transposed:
                pv = lax.dot_general(wb, v_blk[sl[p], :], nt_dims, preferred_element_type=F32)
            else:
                pv = _dot(wb, v_blk[:, sl[p]])
            carry[p] = (c + tl[b, p][:, KEY_BLOCK:], acc + pv)

        n = len(chains)
        skew = n if 2 * tq < SKEW_MIN_ROWS else STAGE_SKEW
        for t in range(n + 2 * skew):
            if t < n:
                scores(*chains[t])
            if 0 <= t - skew < n:
                cumulate(*chains[t - skew])
            if 0 <= t - 2 * skew < n:
                weigh(*chains[t - 2 * skew])
        return tuple(carry)

    def new_block(j, vis):
        if new_keys_t:
            return kn_ref[0, j], vn_ref[0, j], vis, True
        start = pl.multiple_of(j * KEY_BLOCK, KEY_BLOCK)
        return (kn_ref[0, pl.ds(start, KEY_BLOCK), :], vn_ref[0, pl.ds(start, KEY_BLOCK), :], vis,
                False)

    def hist_block(slot):
        hbuf = scratch[0]
        return hbuf[slot, 0].astype(BF16), hbuf[slot, 1].astype(BF16), None, True

    def new_step(js, carry, vis=None):
        return step([new_block(j, vis) for j in js], carry)

    zeros = jnp.zeros((2 * tq, LANES), F32)
    carry = tuple((zeros, zeros) for _ in range(HEAD_PAIRS))

    n_diag = max(tq // KEY_BLOCK, 1)
    if single_q_block:
        first_diag = 0
    else:
        first_diag = i * (tq // KEY_BLOCK) if tq >= KEY_BLOCK else (i * tq) // KEY_BLOCK

    def lead(extra):
        blocks = [new_block(first_diag + d, (first_diag + d) * KEY_BLOCK + col < q_pos)
                  for d in reversed(range(n_diag))]
        blocks += [new_block(first_diag - 1 - e, None) for e in range(extra)]
        if single_q_block:
            blocks += [hist_block(jj) for jj in range(n_lead_hist)]
        return step(blocks, carry)

    if single_q_block:
        for copy in lead_hist_copies(stream):
            copy.wait()
        carry, n_rest = lead(0), 0
    else:
        has_extra = first_diag >= LEAD_EXTRA
        carry = lax.cond(has_extra, lambda: lead(LEAD_EXTRA), lambda: lead(0))
        n_rest = first_diag - jnp.where(has_extra, LEAD_EXTRA, 0)

    def alive(carry):
        c_max = functools.reduce(jnp.maximum, [carry[p][0] for p in pairs])
        return (jnp.max(c_max) > DEAD_LOG2_WEIGHT).astype(jnp.int32)

    def visit(n_blocks, body, live, carry):
        def cond(state):
            return jnp.logical_and(state[0] < n_blocks, state[1] > 0)

        def loop_body(state):
            new_carry = body(state[0], state[2])
            return state[0] + 1, alive(new_carry), new_carry

        return lax.while_loop(cond, loop_body, (jnp.int32(0), live, carry))

    live = alive(carry)
    if not single_q_block:
        _, live, carry = visit(
            lax.shift_right_logical(n_rest, 1),
            lambda jj, carry: new_step([n_rest - 1 - 2 * jj, n_rest - 2 - 2 * jj], carry),
            live, carry)
        _, live, carry = visit(jnp.bitwise_and(n_rest, 1), lambda jj, carry: new_step([0], carry),
                               live, carry)

    partial = hist_valid % KEY_BLOCK

    if hist_dma:
        assert not partial

        def hist_body(jj, carry):
            copies = hist_copies(stream, n_lead_hist + jj, n_lead_hist)
            for copy in copies:
                copy.start()
            for copy in copies:
                copy.wait()
            return step([hist_block(n_lead_hist)], carry)

        _, _, carry = visit(n_hist_blocks - n_lead_hist, hist_body, live, carry)

        @pl.when(stream + 1 < pl.num_programs(0))
        def _():
            for copy in lead_hist_copies(stream + 1):
                copy.start()
    else:
        assert n_hist_blocks == 1
        vis = col < partial if partial else None
        _, _, carry = visit(
            1, lambda jj, carry: step([(kh_ref[0].astype(BF16), vh_ref[0].astype(BF16), vis, True)],
                                      carry),
            live, carry)

    for p in range(HEAD_PAIRS):
        acc = carry[p][1]
        o_ref[0, :, p * LANES:(p + 1) * LANES] = (
            jnp.where(even, acc[:tq], acc[tq:]).astype(o_ref.dtype))


def _cumsum_matrix():
    r = lax.broadcasted_iota(jnp.int32, (2 * KEY_BLOCK, KEY_BLOCK + LANES), 0) % KEY_BLOCK
    c = lax.broadcasted_iota(jnp.int32, (2 * KEY_BLOCK, KEY_BLOCK + LANES), 1)
    return jnp.where((c >= KEY_BLOCK) | (r >= c), -1.0, 0.0).astype(BF16)


def _attention(q, k_new, v_new, k_hist, v_hist, *, new_keys_t, hist_valid, hist_dma, tq):
    b, t, _ = q.shape
    assert t % tq == 0
    if new_keys_t:
        assert k_new.shape[1] * KEY_BLOCK == t
        new_spec = pl.BlockSpec((1,) + k_new.shape[1:], lambda bi, i: (bi, 0, 0, 0))
    else:
        assert k_new.shape[1] % KEY_BLOCK == 0 and k_new.shape[1] >= t
        new_spec = pl.BlockSpec((1,) + k_new.shape[1:], lambda bi, i: (bi, 0, 0))
    hl = k_hist.shape[2]
    assert hl % KEY_BLOCK == 0 and hist_valid <= hl
    if hist_dma:
        hist_spec = pl.BlockSpec(memory_space=pl.ANY)
        scratch = [pltpu.VMEM((HIST_SLOTS, 2, ATTN_W, KEY_BLOCK), k_hist.dtype),
                   pltpu.SemaphoreType.DMA((HIST_SLOTS, 2))]
    else:
        hist_spec = pl.BlockSpec((1, ATTN_W, KEY_BLOCK), lambda bi, i: (0, 0, 0))
        scratch = []
    kern = functools.partial(_attn_kernel, tq=tq, single_q_block=(t == tq), new_keys_t=new_keys_t,
                             n_hist_blocks=hl // KEY_BLOCK, hist_valid=hist_valid,
                             hist_dma=hist_dma)
    return pl.pallas_call(
        kern,
        grid=(b, t // tq),
        in_specs=[pl.BlockSpec((1, tq, ATTN_W), lambda bi, i: (bi, i, 0)),
                  new_spec, new_spec, hist_spec, hist_spec, _whole()],
        out_specs=pl.BlockSpec((1, tq, ATTN_W), lambda bi, i: (bi, i, 0)),
        out_shape=jax.ShapeDtypeStruct((b, t, ATTN_W), BF16),
        scratch_shapes=scratch,
        name="attn",
        compiler_params=pltpu.CompilerParams(
            dimension_semantics=("arbitrary", "arbitrary"),
            vmem_limit_bytes=VMEM_LIMIT_BYTES),
    )(q, k_new, v_new, k_hist, v_hist, _cumsum_matrix())


def _post_kernel(*refs, first_tiles):
    (x1, pooled1, attn1, x2, pooled2, attn2), weights, (y1, y2) = refs[:6], refs[6:-2], refs[-2:]
    step = pl.program_id(0)

    @pl.when(step < first_tiles)
    def _():
        _post_tile(x1, pooled1, attn1, *weights, y1)

    @pl.when(step >= first_tiles)
    def _():
        _post_tile(x2, pooled2, attn2, *weights, y2)


def _post_tile(x_ref, pooled_ref, attn_ref, lng, lnb, wg, bg, wpg, pscale, wbp, wba, wo, bo,
               l1g, l1b, w1, b1, w2, b2, l2g, l2b, y_ref):
    subs = [slice(r, r + POST_SUB) for r in range(0, x_ref.shape[0], POST_SUB)]
    groups = range(len(POOL_WINDOWS))
    h = [_layer_norm(x_ref[rows, :], lng[...], lnb[...]) for rows in subs]
    gates = [_dot(hn.astype(BF16), wg[...]) + bg[:, UQKV_W:] for hn in h]
    pool_out = []
    for rows in subs:
        pooled = pooled_ref[rows, :]
        pool_out.append(jnp.concatenate(
            [_dot(pooled[:, g * POOL_GC:(g + 1) * POOL_GC], wpg[g]) for g in groups],
            axis=-1) * pscale[...])
    branch_pool = [_dot(po.astype(BF16), wbp[...]) for po in pool_out]
    branch_attn = [_dot(attn_ref[rows, :], wba[...]) for rows in subs]
    mix = [jax.nn.sigmoid(g[:, :D_MODEL]) * bp + jax.nn.sigmoid(g[:, D_MODEL:]) * ba
           for g, bp, ba in zip(gates, branch_pool, branch_attn)]
    res = [_dot(m.astype(BF16), wo[...]) + bo[...] for m in mix]
    h1 = [_layer_norm(ALPHA * hn + r, l1g[...], l1b[...]) for hn, r in zip(h, res)]
    hid = [jnp.square(jnp.maximum(_dot(hn.astype(BF16), w1[...]) + b1[...], 0.0)) for hn in h1]
    f = [_dot(hd.astype(BF16), w2[...]) + b2[...] for hd in hid]
    for rows, hn, fn in zip(subs, h1, f):
        y_ref[rows, :] = _layer_norm(ALPHA * hn + fn, l2g[...], l2b[...])


def _post(first, second, weights, tm):
    tm1, tm2 = tm
    n1, n2 = first[0].shape[0] // tm1, second[0].shape[0] // tm2
    assert first[0].shape[0] == n1 * tm1 and second[0].shape[0] == n2 * tm2
    assert tm1 % POST_SUB == 0 and tm2 % POST_SUB == 0
    spec1 = lambda w: pl.BlockSpec((tm1, w), lambda i: (jnp.minimum(i, n1 - 1), 0))
    spec2 = lambda w: pl.BlockSpec((tm2, w), lambda i: (jnp.maximum(i - n1, 0), 0))
    widths = (D_MODEL, POOL_W, ATTN_W)
    return pl.pallas_call(
        functools.partial(_post_kernel, first_tiles=n1),
        grid=(n1 + n2,),
        in_specs=[spec1(w) for w in widths] + [spec2(w) for w in widths]
                 + [_whole() for _ in weights],
        out_specs=(spec1(D_MODEL), spec2(D_MODEL)),
        out_shape=(jax.ShapeDtypeStruct((n1 * tm1, D_MODEL), F32),
                   jax.ShapeDtypeStruct((n2 * tm2, D_MODEL), F32)),
        name="post",
        compiler_params=pltpu.CompilerParams(
            dimension_semantics=("arbitrary",), vmem_limit_bytes=VMEM_LIMIT_BYTES),
    )(*first, *second, *weights)


def _row(v):
    return v.reshape(1, -1).astype(F32)


def _pad_rows(a, rows):
    return jnp.pad(a, ((0, 0), (0, rows - a.shape[1]), (0, 0)))


def _last_rows(halo, u):
    if u.shape[1] >= POOL_STATE:
        return u[None, :, -POOL_STATE:]
    return jnp.concatenate([halo, u], axis=1)[None, :, -POOL_STATE:]


def kernel(x_prompt, x_sample, cache_k, cache_v, state_pool, meta, ln_in_g, ln_in_b, w_in, b_in,
           w_pool_grp, pool_scale, w_br_pool, w_br_attn, w_out, b_out, ln1_g, ln1_b,
           w_mlp1, b_mlp1, w_mlp2, b_mlp2, ln2_g, ln2_b):
    assert w_in.shape[0] == DEPTH
    bp, seq, _ = x_prompt.shape
    bs, dec_seq, _ = x_sample.shape
    past = cache_k.shape[2]

    lng, lnb = _row(ln_in_g), _row(ln_in_b)
    w_in_b = w_in[0].astype(BF16)
    b_in_row = _row(b_in[0])
    post_weights = (
        lng, lnb, w_in_b[:, UQKV_W:], b_in_row, w_pool_grp[0].astype(BF16), _row(pool_scale[0]),
        w_br_pool[0].astype(BF16), w_br_attn[0].astype(BF16), w_out[0].astype(BF16), _row(b_out[0]),
        _row(ln1_g[0]), _row(ln1_b[0]), w_mlp1[0].astype(BF16), _row(b_mlp1[0]),
        w_mlp2[0].astype(BF16), _row(b_mlp2[0]), _row(ln2_g[0]), _row(ln2_b[0]))

    inproj = functools.partial(_inproj, ln_g=lng, ln_b=lnb, w_in_b=w_in_b, b_in_row=b_in_row)
    xp = x_prompt.reshape(bp * seq, D_MODEL)
    xs = x_sample.reshape(bs * dec_seq, D_MODEL)
    u_m, _, _, kf_m, vf_m, kb_m, vb_m = inproj(
        meta.astype(F32), jnp.zeros((1, POOL_HALO, POOL_W), F32), tm=N_META)
    meta_t = lambda rows: jnp.pad(rows.T, ((0, 0), (0, KEY_BLOCK - N_META)))
    pooled_p, q_p, u_tail_p, kt_p, vt_p, ktb_p, vtb_p = _inproj_t(
        xp, lng, lnb, w_in_b, b_in_row,
        w_in_b[:, UQ_W:UQKV_W].T, b_in[0, UQ_W:UQKV_W].reshape(-1, 1).astype(F32),
        u_m, meta_t(kf_m), meta_t(vf_m), streams=bp, tm=512)
    halo_s = jnp.pad(state_pool[0], ((0, 0), (POOL_HALO - POOL_STATE, 0), (0, 0)))
    u_s, pooled_s, q_s, kf_s, vf_s, kb_s, vb_s = inproj(xs, halo_s, tm=512)

    attn_p = _attention(
        q_p.reshape(bp, seq, ATTN_W), ktb_p, vtb_p, meta_t(kb_m)[None], meta_t(vb_m)[None],
        new_keys_t=True, hist_valid=N_META, hist_dma=False, tq=128)

    u_s3 = u_s.reshape(bs, dec_seq, POOL_W)
    tn_s = -(-dec_seq // KEY_BLOCK) * KEY_BLOCK
    keys_on_lanes = lambda c: c.transpose(0, 2, 3, 1).reshape(bs, ATTN_W, past)
    attn_s = _attention(
        q_s.reshape(bs, dec_seq, ATTN_W),
        _pad_rows(kb_s.reshape(bs, dec_seq, ATTN_W), tn_s),
        _pad_rows(vb_s.reshape(bs, dec_seq, ATTN_W), tn_s),
        keys_on_lanes(cache_k[0]), keys_on_lanes(cache_v[0]),
        new_keys_t=False, hist_valid=past, hist_dma=True, tq=dec_seq)

    y_p, y_s = _post(
        (xp, pooled_p, attn_p.reshape(bp * seq, ATTN_W)),
        (xs, pooled_s, attn_s.reshape(bs * dec_seq, ATTN_W)),
        post_weights, tm=(512, 256))

    def seq_major(kv_t):
        full = kv_t.reshape(DEPTH, bp, SB_HEADS, SB_HEAD_DIM, N_META + seq)
        return full.transpose(0, 1, 4, 2, 3)

    assert seq >= POOL_STATE
    heads_s = (DEPTH, bs, dec_seq, SB_HEADS, SB_HEAD_DIM)
    return (
        y_p.reshape(bp, seq, D_MODEL),
        y_s.reshape(bs, dec_seq, D_MODEL),
        seq_major(kt_p),
        seq_major(vt_p),
        u_tail_p[None, :, POOL_HALO - POOL_STATE:],
        kf_s.reshape(heads_s),
        vf_s.reshape(heads_s),
        _last_rows(halo_s, u_s3),
    )
```

```python
import functools

import jax
import jax.numpy as jnp
from jax import lax
from jax.experimental import pallas as pl
from jax.experimental.pallas import tpu as pltpu

D_MODEL = 1024
N_META = 16
POOL_W = D_MODEL // 2
POOL_WINDOWS = (2, 4, 8, 16)
POOL_GC = POOL_W // len(POOL_WINDOWS)
POOL_STATE = max(POOL_WINDOWS) - 1
POOL_HALO = POOL_STATE + 1
SB_HEADS = 8
SB_HEAD_DIM = 64
ATTN_W = SB_HEADS * SB_HEAD_DIM
D_FF = 4 * D_MODEL
UQ_W = POOL_W + ATTN_W
UQKV_W = POOL_W + 3 * ATTN_W
LN_EPS = 1e-5
DEPTH = 1
ALPHA = (2.0 * DEPTH) ** 0.25
SB_SCALE = SB_HEAD_DIM ** -0.5
LOG2_E = 1.4426950408889634
Q_SCALE = SB_SCALE * LOG2_E

LANES = 128
KEY_BLOCK = LANES
HEAD_PAIRS = ATTN_W // LANES
STAGE_SKEW = 2
SKEW_MIN_ROWS = 256
HIST_LEAD = 2
HIST_SLOTS = HIST_LEAD + 1
LEAD_EXTRA = 2
POST_SUB = 256
INPROJ_SUB = 256
VMEM_LIMIT_BYTES = 56 * 1024 * 1024
DEAD_LOG2_WEIGHT = -160.0

F32 = jnp.float32
BF16 = jnp.bfloat16


def _layer_norm(x, g, b):
    mu = jnp.mean(x, axis=-1, keepdims=True)
    xc = x - mu
    var = jnp.mean(xc * xc, axis=-1, keepdims=True)
    return xc * lax.rsqrt(var + LN_EPS) * g + b


def _dot(a, b):
    return jnp.dot(a, b, preferred_element_type=F32)


def _whole():
    return pl.BlockSpec(memory_space=pltpu.VMEM)


def _pooled(halo, u):
    full = jnp.concatenate([halo, u], axis=0)
    outs = []
    for g, w in enumerate(POOL_WINDOWS):
        x = full[:, g * POOL_GC:(g + 1) * POOL_GC]
        s = x
        span = 1
        while span < w:
            s = s + pltpu.roll(s, span, axis=0)
            span *= 2
        outs.append(s[POOL_HALO:] * (1.0 / w) - x[POOL_HALO:])
    return jnp.concatenate(outs, axis=-1)


def _inproj_kernel(x_ref, halo_ref, g_ref, b_ref, w_ref, bias_ref,
                   u_ref, pooled_ref, q_ref, kf_ref, vf_ref, kb_ref, vb_ref):
    h = _layer_norm(x_ref[...], g_ref[...], b_ref[...])
    z = _dot(h.astype(BF16), w_ref[:, :UQKV_W]) + bias_ref[:, :UQKV_W]
    u = z[:, :POOL_W]
    u_ref[...] = u
    t = x_ref.shape[0] // halo_ref.shape[0]
    for s in range(halo_ref.shape[0]):
        rows = slice(s * t, (s + 1) * t)
        pooled_ref[rows, :] = _pooled(halo_ref[s], u[rows]).astype(BF16)
    q_ref[...] = (z[:, POOL_W:POOL_W + ATTN_W] * Q_SCALE).astype(BF16)
    k = z[:, POOL_W + ATTN_W:POOL_W + 2 * ATTN_W]
    v = z[:, POOL_W + 2 * ATTN_W:]
    kf_ref[...] = k
    vf_ref[...] = v
    kb_ref[...] = k.astype(BF16)
    vb_ref[...] = v.astype(BF16)


def _inproj(x, halo, ln_g, ln_b, w_in_b, b_in_row, tm):
    rows = x.shape[0]
    t = rows // halo.shape[0]
    assert rows % tm == 0 and tm % t == 0 and rows == t * halo.shape[0]
    row_spec = lambda w: pl.BlockSpec((tm, w), lambda i: (i, 0))
    out_shape = (
        jax.ShapeDtypeStruct((rows, POOL_W), F32),
        jax.ShapeDtypeStruct((rows, POOL_W), BF16),
        jax.ShapeDtypeStruct((rows, ATTN_W), BF16),
        jax.ShapeDtypeStruct((rows, ATTN_W), F32),
        jax.ShapeDtypeStruct((rows, ATTN_W), F32),
        jax.ShapeDtypeStruct((rows, ATTN_W), BF16),
        jax.ShapeDtypeStruct((rows, ATTN_W), BF16),
    )
    return pl.pallas_call(
        _inproj_kernel,
        grid=(rows // tm,),
        in_specs=[row_spec(D_MODEL),
                  pl.BlockSpec((tm // t, POOL_HALO, POOL_W), lambda i: (i, 0, 0)),
                  _whole(), _whole(), _whole(), _whole()],
        out_specs=tuple(row_spec(s.shape[1]) for s in out_shape),
        out_shape=out_shape,
        name="inproj",
        compiler_params=pltpu.CompilerParams(
            dimension_semantics=("arbitrary",), vmem_limit_bytes=VMEM_LIMIT_BYTES),
    )(x, halo, ln_g, ln_b, w_in_b, b_in_row)


def _inproj_t_kernel(x_ref, g_ref, b_ref, w_uq_ref, b_uq_ref, w_kvt_ref, b_kvt_ref,
                     u_meta_ref, kt_meta_ref, vt_meta_ref,
                     pooled_ref, q_ref, u_tail_ref, kt_ref, vt_ref, ktb_ref, vtb_ref,
                     u_halo, kt_carry, vt_carry):
    j = pl.program_id(1)
    tm = x_ref.shape[0]
    t = kt_ref.shape[2] - N_META
    lane = lax.broadcasted_iota(jnp.int32, (ATTN_W, LANES), 1)

    @pl.when(j == 0)
    def _():
        u_halo[...] = u_meta_ref[...]
        kt_carry[...] = kt_meta_ref[...]
        vt_carry[...] = vt_meta_ref[...]

    def shifted(carry, cur):
        rolled = pltpu.roll(cur, N_META, axis=1)
        out = jnp.concatenate(
            [jnp.where(lane < N_META, carry, rolled[:, :LANES]), rolled[:, LANES:]], axis=1)
        return out, rolled[:, :LANES]

    subs = [slice(r, r + INPROJ_SUB) for r in range(0, tm, INPROJ_SUB)]
    hb = [_layer_norm(x_ref[rows, :], g_ref[...], b_ref[...]).astype(BF16) for rows in subs]
    z, zt = [], []
    for h in hb:
        z.append(_dot(h, w_uq_ref[:, :UQ_W]))
        zt.append(lax.dot_general(w_kvt_ref[...], h, (((1,), (1,)), ((), ())),
                                  preferred_element_type=F32))
    halo, kt_c, vt_c = u_halo[...], kt_carry[...], vt_carry[...]
    for n, rows in enumerate(subs):
        zn = z[n] + b_uq_ref[:, :UQ_W]
        u = zn[:, :POOL_W]
        q_ref[rows, :] = (zn[:, POOL_W:] * Q_SCALE).astype(BF16)
        pooled_ref[rows, :] = _pooled(halo, u).astype(BF16)
        halo = u[INPROJ_SUB - POOL_HALO:]
        ztn = zt[n] + b_kvt_ref[...]
        kt, vt = ztn[:ATTN_W], ztn[ATTN_W:]
        for c in range(INPROJ_SUB // KEY_BLOCK):
            cols = slice(c * KEY_BLOCK, (c + 1) * KEY_BLOCK)
            blk = rows.start // KEY_BLOCK + c
            ktb_ref[0, blk] = kt[:, cols].astype(BF16)
            vtb_ref[0, blk] = vt[:, cols].astype(BF16)
        out_cols = pl.ds(pl.multiple_of(j * tm + rows.start, LANES), INPROJ_SUB)
        kt_ref[0, :, out_cols], kt_c = shifted(kt_c, kt)
        vt_ref[0, :, out_cols], vt_c = shifted(vt_c, vt)
    u_halo[...] = halo
    u_tail_ref[0] = halo
    kt_carry[...] = kt_c
    vt_carry[...] = vt_c

    @pl.when(j == pl.num_programs(1) - 1)
    def _():
        kt_ref[0, :, t:] = kt_c[:, :N_META]
        vt_ref[0, :, t:] = vt_c[:, :N_META]


def _inproj_t(x, ln_g, ln_b, w_uq, b_uq, w_kvt, b_kvt, u_meta, kt_meta, vt_meta, streams, tm):
    rows = x.shape[0]
    t = rows // streams
    assert rows == streams * t and t % tm == 0 and POOL_HALO == N_META
    assert tm % INPROJ_SUB == 0 and INPROJ_SUB % KEY_BLOCK == 0
    per = t // tm
    kb = tm // KEY_BLOCK
    row_spec = lambda w: pl.BlockSpec((tm, w), lambda s, j: (s * per + j, 0))
    t_spec = pl.BlockSpec((1, ATTN_W, N_META + t), lambda s, j: (s, 0, 0))
    tb_spec = pl.BlockSpec((1, kb, ATTN_W, KEY_BLOCK), lambda s, j: (s, j, 0, 0))
    out_shape = (
        jax.ShapeDtypeStruct((rows, POOL_W), BF16),
        jax.ShapeDtypeStruct((rows, ATTN_W), BF16),
        jax.ShapeDtypeStruct((streams, POOL_HALO, POOL_W), F32),
        jax.ShapeDtypeStruct((streams, ATTN_W, N_META + t), F32),
        jax.ShapeDtypeStruct((streams, ATTN_W, N_META + t), F32),
        jax.ShapeDtypeStruct((streams, t // KEY_BLOCK, ATTN_W, KEY_BLOCK), BF16),
        jax.ShapeDtypeStruct((streams, t // KEY_BLOCK, ATTN_W, KEY_BLOCK), BF16),
    )
    return pl.pallas_call(
        _inproj_t_kernel,
        grid=(streams, per),
        in_specs=[row_spec(D_MODEL)] + [_whole()] * 9,
        out_specs=(row_spec(POOL_W), row_spec(ATTN_W),
                   pl.BlockSpec((1, POOL_HALO, POOL_W), lambda s, j: (s, 0, 0)),
                   t_spec, t_spec, tb_spec, tb_spec),
        out_shape=out_shape,
        scratch_shapes=[pltpu.VMEM((POOL_HALO, POOL_W), F32),
                        pltpu.VMEM((ATTN_W, LANES), F32), pltpu.VMEM((ATTN_W, LANES), F32)],
        name="inproj_t",
        compiler_params=pltpu.CompilerParams(
            dimension_semantics=("arbitrary", "arbitrary"), vmem_limit_bytes=VMEM_LIMIT_BYTES),
    )(x, ln_g, ln_b, w_uq, b_uq, w_kvt, b_kvt, u_meta, kt_meta, vt_meta)


def _attn_kernel(q_ref, kn_ref, vn_ref, kh_ref, vh_ref, su_ref, o_ref, *scratch,
                 tq, single_q_block, new_keys_t, n_hist_blocks, hist_valid, hist_dma):
    stream = pl.program_id(0)
    i = pl.program_id(1)

    n_lead_hist = min(HIST_LEAD, n_hist_blocks) if hist_dma else 0

    def hist_copies(of_stream, jj, slot):
        hbuf, hsem = scratch[2:]
        start = pl.multiple_of((n_hist_blocks - 1 - jj) * KEY_BLOCK, KEY_BLOCK)
        return [pltpu.make_async_copy(src.at[of_stream, :, pl.ds(start, KEY_BLOCK)],
                                      hbuf.at[slot, which], hsem.at[slot, which])
                for which, src in enumerate((kh_ref, vh_ref))]

    def lead_hist_copies(of_stream):
        return [copy for jj in range(n_lead_hist) for copy in hist_copies(of_stream, jj, jj)]

    if hist_dma:
        assert single_q_block

        @pl.when(stream == 0)
        def _():
            for copy in lead_hist_copies(stream):
                copy.start()

    lane = lax.broadcasted_iota(jnp.int32, (tq, LANES), 1)
    even = lane < SB_HEAD_DIM
    qs = []
    for p in range(HEAD_PAIRS):
        q = q_ref[0, :, p * LANES:(p + 1) * LANES]
        zero = jnp.zeros_like(q)
        qs.append(jnp.concatenate([jnp.where(even, q, zero), jnp.where(even, zero, q)], axis=0))
    su = su_ref[...]

    row = lax.broadcasted_iota(jnp.int32, (2 * tq, KEY_BLOCK), 0)
    col = lax.broadcasted_iota(jnp.int32, (2 * tq, KEY_BLOCK), 1)
    q_pos = jnp.where(row >= tq, row - tq, row) + i * tq

    pairs = range(HEAD_PAIRS)

    nt_dims = (((1,), (1,)), ((), ()))

    c_ref, acc_ref = scratch[:2]

    def step(blocks, first=False):
        sl = [slice(p * LANES, (p + 1) * LANES) for p in pairs]
        chains = [(b, p) for b in range(len(blocks)) for p in pairs]
        s, tl, carry = {}, {}, {}

        def scores(b, p):
            k_blk, _, _, transposed = blocks[b]
            if transposed:
                s[b, p] = _dot(qs[p], k_blk[sl[p], :])
            else:
                s[b, p] = lax.dot_general(qs[p], k_blk[:, sl[p]], nt_dims,
                                          preferred_element_type=F32)

        def cumulate(b, p):
            vis = blocks[b][2]
            z2 = s[b, p]
            sp = jnp.maximum(z2, 0.0) + jnp.log(1.0 + jnp.exp2(-jnp.abs(z2))) * LOG2_E
            if vis is not None:
                sp = jnp.where(vis, sp, 0.0)
            hi = sp.astype(BF16)
            lo = (sp - hi.astype(F32)).astype(BF16)
            tl[b, p] = _dot(jnp.concatenate([hi, lo], axis=1), su)

        def weigh(b, p):
            _, v_blk, vis, transposed = blocks[b]
            if b > 0:
                c, acc = carry[p]
            elif first:
                c = acc = jnp.zeros((2 * tq, LANES), F32)
            else:
                c, acc = c_ref[p], acc_ref[p]
            w = jnp.exp2(s[b, p] + tl[b, p][:, :KEY_BLOCK] + c)
            if vis is not None:
                w = jnp.where(vis, w, 0.0)
            wb = w.astype(BF16)
            if transposed:
                pv = lax.dot_general(wb, v_blk[sl[p], :], nt_dims, preferred_element_type=F32)
            else:
                pv = _dot(wb, v_blk[:, sl[p]])
            carry[p] = (c + tl[b, p][:, KEY_BLOCK:], acc + pv)
            if b == len(blocks) - 1:
                c_ref[p], acc_ref[p] = carry[p]

        n = len(chains)
        skew = n if 2 * tq < SKEW_MIN_ROWS else STAGE_SKEW
        for t in range(n + 2 * skew):
            if t < n:
                scores(*chains[t])
            if 0 <= t - skew < n:
                cumulate(*chains[t - skew])
            if 0 <= t - 2 * skew < n:
                weigh(*chains[t - 2 * skew])
        c_max = functools.reduce(jnp.maximum, [carry[p][0] for p in pairs])
        return (jnp.max(c_max) > DEAD_LOG2_WEIGHT).astype(jnp.int32)

    def new_block(j, vis):
        if new_keys_t:
            return kn_ref[0, j], vn_ref[0, j], vis, True
        start = pl.multiple_of(j * KEY_BLOCK, KEY_BLOCK)
        return (kn_ref[0, pl.ds(start, KEY_BLOCK), :], vn_ref[0, pl.ds(start, KEY_BLOCK), :], vis,
                False)

    def hist_block(slot):
        hbuf = scratch[2]
        return hbuf[slot, 0].astype(BF16), hbuf[slot, 1].astype(BF16), None, True

    def new_step(js):
        return step([new_block(j, None) for j in js])

    n_diag = max(tq // KEY_BLOCK, 1)
    if single_q_block:
        first_diag = 0
    else:
        first_diag = i * (tq // KEY_BLOCK) if tq >= KEY_BLOCK else (i * tq) // KEY_BLOCK

    def lead(extra):
        blocks = [new_block(first_diag + d, (first_diag + d) * KEY_BLOCK + col < q_pos)
                  for d in reversed(range(n_diag))]
        blocks += [new_block(first_diag - 1 - e, None) for e in range(extra)]
        if single_q_block:
            blocks += [hist_block(jj) for jj in range(n_lead_hist)]
        return step(blocks, first=True)

    if single_q_block:
        for copy in lead_hist_copies(stream):
            copy.wait()
        live, n_rest = lead(0), 0
    else:
        has_extra = first_diag >= LEAD_EXTRA
        live = lax.cond(has_extra, lambda: lead(LEAD_EXTRA), lambda: lead(0))
        n_rest = first_diag - jnp.where(has_extra, LEAD_EXTRA, 0)

    def visit(n_blocks, body, live):
        def cond(state):
            return jnp.logical_and(state[0] < n_blocks, state[1] > 0)

        return lax.while_loop(cond, lambda state: (state[0] + 1, body(state[0])),
                              (jnp.int32(0), live))[1]

    if not single_q_block:
        live = visit(lax.shift_right_logical(n_rest, 1),
                     lambda jj: new_step([n_rest - 1 - 2 * jj, n_rest - 2 - 2 * jj]), live)
        live = visit(jnp.bitwise_and(n_rest, 1), lambda jj: new_step([0]), live)

    partial = hist_valid % KEY_BLOCK

    if hist_dma:
        assert not partial

        def hist_body(jj):
            copies = hist_copies(stream, n_lead_hist + jj, n_lead_hist)
            for copy in copies:
                copy.start()
            for copy in copies:
                copy.wait()
            return step([hist_block(n_lead_hist)])

        visit(n_hist_blocks - n_lead_hist, hist_body, live)

        @pl.when(stream + 1 < pl.num_programs(0))
        def _():
            for copy in lead_hist_copies(stream + 1):
                copy.start()
    else:
        assert n_hist_blocks == 1
        vis = col < partial if partial else None
        visit(1, lambda jj: step([(kh_ref[0].astype(BF16), vh_ref[0].astype(BF16), vis, True)]),
              live)

    for p in range(HEAD_PAIRS):
        acc = acc_ref[p]
        o_ref[0, :, p * LANES:(p + 1) * LANES] = (
            jnp.where(even, acc[:tq], acc[tq:]).astype(o_ref.dtype))


def _cumsum_matrix():
    r = lax.broadcasted_iota(jnp.int32, (2 * KEY_BLOCK, KEY_BLOCK + LANES), 0) % KEY_BLOCK
    c = lax.broadcasted_iota(jnp.int32, (2 * KEY_BLOCK, KEY_BLOCK + LANES), 1)
    return jnp.where((c >= KEY_BLOCK) | (r >= c), -1.0, 0.0).astype(BF16)


def _attention(q, k_new, v_new, k_hist, v_hist, *, new_keys_t, hist_valid, hist_dma, tq):
    b, t, _ = q.shape
    assert t % tq == 0
    if new_keys_t:
        assert k_new.shape[1] * KEY_BLOCK == t
        new_spec = pl.BlockSpec((1,) + k_new.shape[1:], lambda bi, i: (bi, 0, 0, 0))
    else:
        assert k_new.shape[1] % KEY_BLOCK == 0 and k_new.shape[1] >= t
        new_spec = pl.BlockSpec((1,) + k_new.shape[1:], lambda bi, i: (bi, 0, 0))
    hl = k_hist.shape[2]
    assert hl % KEY_BLOCK == 0 and hist_valid <= hl
    scratch = [pltpu.VMEM((HEAD_PAIRS, 2 * tq, LANES), F32)] * 2
    if hist_dma:
        hist_spec = pl.BlockSpec(memory_space=pl.ANY)
        scratch += [pltpu.VMEM((HIST_SLOTS, 2, ATTN_W, KEY_BLOCK), k_hist.dtype),
                    pltpu.SemaphoreType.DMA((HIST_SLOTS, 2))]
    else:
        hist_spec = pl.BlockSpec((1, ATTN_W, KEY_BLOCK), lambda bi, i: (0, 0, 0))
    kern = functools.partial(_attn_kernel, tq=tq, single_q_block=(t == tq), new_keys_t=new_keys_t,
                             n_hist_blocks=hl // KEY_BLOCK, hist_valid=hist_valid,
                             hist_dma=hist_dma)
    return pl.pallas_call(
        kern,
        grid=(b, t // tq),
        in_specs=[pl.BlockSpec((1, tq, ATTN_W), lambda bi, i: (bi, i, 0)),
                  new_spec, new_spec, hist_spec, hist_spec, _whole()],
        out_specs=pl.BlockSpec((1, tq, ATTN_W), lambda bi, i: (bi, i, 0)),
        out_shape=jax.ShapeDtypeStruct((b, t, ATTN_W), BF16),
        scratch_shapes=scratch,
        name="attn",
        compiler_params=pltpu.CompilerParams(
            dimension_semantics=("arbitrary", "arbitrary"),
            vmem_limit_bytes=VMEM_LIMIT_BYTES),
    )(q, k_new, v_new, k_hist, v_hist, _cumsum_matrix())


def _post_kernel(*refs, first_tiles):
    (x1, pooled1, attn1, x2, pooled2, attn2), weights, (y1, y2) = refs[:6], refs[6:-2], refs[-2:]
    step = pl.program_id(0)

    @pl.when(step < first_tiles)
    def _():
        _post_tile(x1, pooled1, attn1, *weights, y1)

    @pl.when(step >= first_tiles)
    def _():
        _post_tile(x2, pooled2, attn2, *weights, y2)


def _post_tile(x_ref, pooled_ref, attn_ref, lng, lnb, wg, bg, wpg, pscale, wbp, wba, wo, bo,
               l1g, l1b, w1, b1, w2, b2, l2g, l2b, y_ref):
    subs = [slice(r, r + POST_SUB) for r in range(0, x_ref.shape[0], POST_SUB)]
    groups = range(len(POOL_WINDOWS))
    h = [_layer_norm(x_ref[rows, :], lng[...], lnb[...]) for rows in subs]
    gates = [_dot(hn.astype(BF16), wg[...]) + bg[:, UQKV_W:] for hn in h]
    pool_out = []
    for rows in subs:
        pooled = pooled_ref[rows, :]
        pool_out.append(jnp.concatenate(
            [_dot(pooled[:, g * POOL_GC:(g + 1) * POOL_GC], wpg[g]) for g in groups],
            axis=-1) * pscale[...])
    branch_pool = [_dot(po.astype(BF16), wbp[...]) for po in pool_out]
    branch_attn = [_dot(attn_ref[rows, :], wba[...]) for rows in subs]
    mix = [jax.nn.sigmoid(g[:, :D_MODEL]) * bp + jax.nn.sigmoid(g[:, D_MODEL:]) * ba
           for g, bp, ba in zip(gates, branch_pool, branch_attn)]
    res = [_dot(m.astype(BF16), wo[...]) + bo[...] for m in mix]
    h1 = [_layer_norm(ALPHA * hn + r, l1g[...], l1b[...]) for hn, r in zip(h, res)]
    hid = [jnp.square(jnp.maximum(_dot(hn.astype(BF16), w1[...]) + b1[...], 0.0)) for hn in h1]
    f = [_dot(hd.astype(BF16), w2[...]) + b2[...] for hd in hid]
    for rows, hn, fn in zip(subs, h1, f):
        y_ref[rows, :] = _layer_norm(ALPHA * hn + fn, l2g[...], l2b[...])


def _post(first, second, weights, tm):
    tm1, tm2 = tm
    n1, n2 = first[0].shape[0] // tm1, second[0].shape[0] // tm2
    assert first[0].shape[0] == n1 * tm1 and second[0].shape[0] == n2 * tm2
    assert tm1 % POST_SUB == 0 and tm2 % POST_SUB == 0
    spec1 = lambda w: pl.BlockSpec((tm1, w), lambda i: (jnp.minimum(i, n1 - 1), 0))
    spec2 = lambda w: pl.BlockSpec((tm2, w), lambda i: (jnp.maximum(i - n1, 0), 0))
    widths = (D_MODEL, POOL_W, ATTN_W)
    return pl.pallas_call(
        functools.partial(_post_kernel, first_tiles=n1),
        grid=(n1 + n2,),
        in_specs=[spec1(w) for w in widths] + [spec2(w) for w in widths]
                 + [_whole() for _ in weights],
        out_specs=(spec1(D_MODEL), spec2(D_MODEL)),
        out_shape=(jax.ShapeDtypeStruct((n1 * tm1, D_MODEL), F32),
                   jax.ShapeDtypeStruct((n2 * tm2, D_MODEL), F32)),
        name="post",
        compiler_params=pltpu.CompilerParams(
            dimension_semantics=("arbitrary",), vmem_limit_bytes=VMEM_LIMIT_BYTES),
    )(*first, *second, *weights)


def _row(v):
    return v.reshape(1, -1).astype(F32)


def _pad_rows(a, rows):
    return jnp.pad(a, ((0, 0), (0, rows - a.shape[1]), (0, 0)))


def _last_rows(halo, u):
    if u.shape[1] >= POOL_STATE:
        return u[None, :, -POOL_STATE:]
    return jnp.concatenate([halo, u], axis=1)[None, :, -POOL_STATE:]


def kernel(x_prompt, x_sample, cache_k, cache_v, state_pool, meta, ln_in_g, ln_in_b, w_in, b_in,
           w_pool_grp, pool_scale, w_br_pool, w_br_attn, w_out, b_out, ln1_g, ln1_b,
           w_mlp1, b_mlp1, w_mlp2, b_mlp2, ln2_g, ln2_b):
    assert w_in.shape[0] == DEPTH
    bp, seq, _ = x_prompt.shape
    bs, dec_seq, _ = x_sample.shape
    past = cache_k.shape[2]

    lng, lnb = _row(ln_in_g), _row(ln_in_b)
    w_in_b = w_in[0].astype(BF16)
    b_in_row = _row(b_in[0])
    post_weights = (
        lng, lnb, w_in_b[:, UQKV_W:], b_in_row, w_pool_grp[0].astype(BF16), _row(pool_scale[0]),
        w_br_pool[0].astype(BF16), w_br_attn[0].astype(BF16), w_out[0].astype(BF16), _row(b_out[0]),
        _row(ln1_g[0]), _row(ln1_b[0]), w_mlp1[0].astype(BF16), _row(b_mlp1[0]),
        w_mlp2[0].astype(BF16), _row(b_mlp2[0]), _row(ln2_g[0]), _row(ln2_b[0]))

    inproj = functools.partial(_inproj, ln_g=lng, ln_b=lnb, w_in_b=w_in_b, b_in_row=b_in_row)
    xp = x_prompt.reshape(bp * seq, D_MODEL)
    xs = x_sample.reshape(bs * dec_seq, D_MODEL)
    u_m, _, _, kf_m, vf_m, kb_m, vb_m = inproj(
        meta.astype(F32), jnp.zeros((1, POOL_HALO, POOL_W), F32), tm=N_META)
    meta_t = lambda rows: jnp.pad(rows.T, ((0, 0), (0, KEY_BLOCK - N_META)))
    pooled_p, q_p, u_tail_p, kt_p, vt_p, ktb_p, vtb_p = _inproj_t(
        xp, lng, lnb, w_in_b, b_in_row,
        w_in_b[:, UQ_W:UQKV_W].T, b_in[0, UQ_W:UQKV_W].reshape(-1, 1).astype(F32),
        u_m, meta_t(kf_m), meta_t(vf_m), streams=bp, tm=512)
    halo_s = jnp.pad(state_pool[0], ((0, 0), (POOL_HALO - POOL_STATE, 0), (0, 0)))
    u_s, pooled_s, q_s, kf_s, vf_s, kb_s, vb_s = inproj(xs, halo_s, tm=512)

    attn_p = _attention(
        q_p.reshape(bp, seq, ATTN_W), ktb_p, vtb_p, meta_t(kb_m)[None], meta_t(vb_m)[None],
        new_keys_t=True, hist_valid=N_META, hist_dma=False, tq=128)

    u_s3 = u_s.reshape(bs, dec_seq, POOL_W)
    tn_s = -(-dec_seq // KEY_BLOCK) * KEY_BLOCK
    keys_on_lanes = lambda c: c.transpose(0, 2, 3, 1).reshape(bs, ATTN_W, past)
    attn_s = _attention(
        q_s.reshape(bs, dec_seq, ATTN_W),
        _pad_rows(kb_s.reshape(bs, dec_seq, ATTN_W), tn_s),
        _pad_rows(vb_s.reshape(bs, dec_seq, ATTN_W), tn_s),
        keys_on_lanes(cache_k[0]), keys_on_lanes(cache_v[0]),
        new_keys_t=False, hist_valid=past, hist_dma=True, tq=dec_seq)

    y_p, y_s = _post(
        (xp, pooled_p, attn_p.reshape(bp * seq, ATTN_W)),
        (xs, pooled_s, attn_s.reshape(bs * dec_seq, ATTN_W)),
        post_weights, tm=(512, 256))

    def seq_major(kv_t):
        full = kv_t.reshape(DEPTH, bp, SB_HEADS, SB_HEAD_DIM, N_META + seq)
        return full.transpose(0, 1, 4, 2, 3)

    assert seq >= POOL_STATE
    heads_s = (DEPTH, bs, dec_seq, SB_HEADS, SB_HEAD_DIM)
    return (
        y_p.reshape(bp, seq, D_MODEL),
        y_s.reshape(bs, dec_seq, D_MODEL),
        seq_major(kt_p),
        seq_major(vt_p),
        u_tail_p[None, :, POOL_HALO - POOL_STATE:],
        kf_s.reshape(heads_s),
        vf_s.reshape(heads_s),
        _last_rows(halo_s, u_s3),
    )
```

```python
import functools

import jax
import jax.numpy as jnp
from jax import lax
from jax.experimental import pallas as pl
from jax.experimental.pallas import tpu as pltpu

D_MODEL = 1024
N_META = 16
POOL_W = D_MODEL // 2
POOL_WINDOWS = (2, 4, 8, 16)
POOL_GC = POOL_W // len(POOL_WINDOWS)
POOL_STATE = max(POOL_WINDOWS) - 1
POOL_HALO = POOL_STATE + 1
SB_HEADS = 8
SB_HEAD_DIM = 64
ATTN_W = SB_HEADS * SB_HEAD_DIM
D_FF = 4 * D_MODEL
UQ_W = POOL_W + ATTN_W
UQKV_W = POOL_W + 3 * ATTN_W
LN_EPS = 1e-5
DEPTH = 1
ALPHA = (2.0 * DEPTH) ** 0.25
SB_SCALE = SB_HEAD_DIM ** -0.5
LOG2_E = 1.4426950408889634
Q_SCALE = SB_SCALE * LOG2_E

LANES = 128
KEY_BLOCK = LANES
HEAD_PAIRS = ATTN_W // LANES
STAGE_SKEW = 2
SKEW_MIN_ROWS = 256
HIST_LEAD = 2
HIST_SLOTS = 2 * HIST_LEAD + 1
LEAD_EXTRA = 2
POST_SUB = 256
INPROJ_SUB = 256
VMEM_LIMIT_BYTES = 56 * 1024 * 1024
DEAD_LOG2_WEIGHT = -160.0

F32 = jnp.float32
BF16 = jnp.bfloat16


def _layer_norm(x, g, b):
    mu = jnp.mean(x, axis=-1, keepdims=True)
    xc = x - mu
    var = jnp.mean(xc * xc, axis=-1, keepdims=True)
    return xc * lax.rsqrt(var + LN_EPS) * g + b


def _dot(a, b):
    return jnp.dot(a, b, preferred_element_type=F32)


def _whole():
    return pl.BlockSpec(memory_space=pltpu.VMEM)


def _pooled(halo, u):
    full = jnp.concatenate([halo, u], axis=0)
    outs = []
    for g, w in enumerate(POOL_WINDOWS):
        x = full[:, g * POOL_GC:(g + 1) * POOL_GC]
        s = x
        span = 1
        while span < w:
            s = s + pltpu.roll(s, span, axis=0)
            span *= 2
        outs.append(s[POOL_HALO:] * (1.0 / w) - x[POOL_HALO:])
    return jnp.concatenate(outs, axis=-1)


def _inproj_kernel(x_ref, halo_ref, g_ref, b_ref, w_ref, bias_ref,
                   u_ref, pooled_ref, q_ref, kf_ref, vf_ref, kb_ref, vb_ref):
    h = _layer_norm(x_ref[...], g_ref[...], b_ref[...])
    z = _dot(h.astype(BF16), w_ref[...]) + bias_ref[:, :UQKV_W]
    u = z[:, :POOL_W]
    u_ref[...] = u
    q_ref[...] = (z[:, POOL_W:POOL_W + ATTN_W] * Q_SCALE).astype(BF16)
    k = z[:, POOL_W + ATTN_W:POOL_W + 2 * ATTN_W]
    v = z[:, POOL_W + 2 * ATTN_W:]
    kf_ref[...] = k
    vf_ref[...] = v
    t = x_ref.shape[0] // halo_ref.shape[0]
    pad = jnp.zeros((kb_ref.shape[1] - t, ATTN_W), BF16)
    for s in range(halo_ref.shape[0]):
        rows = slice(s * t, (s + 1) * t)
        pooled_ref[rows, :] = _pooled(halo_ref[s], u[rows]).astype(BF16)
        kb_ref[s] = jnp.concatenate([k[rows].astype(BF16), pad], axis=0)
        vb_ref[s] = jnp.concatenate([v[rows].astype(BF16), pad], axis=0)


def _inproj(x, halo, ln_g, ln_b, w_uqkv, b_in_row, tm):
    rows = x.shape[0]
    streams = halo.shape[0]
    t = rows // streams
    tn = -(-t // KEY_BLOCK) * KEY_BLOCK
    assert rows % tm == 0 and tm % t == 0 and rows == t * streams
    row_spec = lambda w: pl.BlockSpec((tm, w), lambda i: (i, 0))
    stream_spec = lambda r, w: pl.BlockSpec((tm // t, r, w), lambda i: (i, 0, 0))
    out_shape = (
        jax.ShapeDtypeStruct((rows, POOL_W), F32),
        jax.ShapeDtypeStruct((rows, POOL_W), BF16),
        jax.ShapeDtypeStruct((rows, ATTN_W), BF16),
        jax.ShapeDtypeStruct((rows, ATTN_W), F32),
        jax.ShapeDtypeStruct((rows, ATTN_W), F32),
        jax.ShapeDtypeStruct((streams, tn, ATTN_W), BF16),
        jax.ShapeDtypeStruct((streams, tn, ATTN_W), BF16),
    )
    return pl.pallas_call(
        _inproj_kernel,
        grid=(rows // tm,),
        in_specs=[row_spec(D_MODEL), stream_spec(POOL_HALO, POOL_W),
                  _whole(), _whole(), _whole(), _whole()],
        out_specs=tuple(row_spec(s.shape[1]) for s in out_shape[:5])
                  + (stream_spec(tn, ATTN_W), stream_spec(tn, ATTN_W)),
        out_shape=out_shape,
        name="inproj",
        compiler_params=pltpu.CompilerParams(
            dimension_semantics=("arbitrary",), vmem_limit_bytes=VMEM_LIMIT_BYTES),
    )(x, halo, ln_g, ln_b, w_uqkv, b_in_row)


def _inproj_t_kernel(x_ref, g_ref, b_ref, w_uq_ref, b_uq_ref, w_kvt_ref, b_kvt_ref,
                     u_meta_ref, kt_meta_ref, vt_meta_ref,
                     pooled_ref, q_ref, u_tail_ref, kt_ref, vt_ref, ktb_ref, vtb_ref,
                     u_halo, kt_carry, vt_carry):
    j = pl.program_id(1)
    tm = x_ref.shape[0]
    t = kt_ref.shape[2] - N_META
    lane = lax.broadcasted_iota(jnp.int32, (ATTN_W, LANES), 1)

    @pl.when(j == 0)
    def _():
        u_halo[...] = u_meta_ref[...]
        kt_carry[...] = kt_meta_ref[...]
        vt_carry[...] = vt_meta_ref[...]

    def shifted(carry, cur):
        rolled = pltpu.roll(cur, N_META, axis=1)
        out = jnp.concatenate(
            [jnp.where(lane < N_META, carry, rolled[:, :LANES]), rolled[:, LANES:]], axis=1)
        return out, rolled[:, :LANES]

    subs = [slice(r, r + INPROJ_SUB) for r in range(0, tm, INPROJ_SUB)]
    hb = [_layer_norm(x_ref[rows, :], g_ref[...], b_ref[...]).astype(BF16) for rows in subs]
    z, zt = [], []
    for h in hb:
        z.append(_dot(h, w_uq_ref[:, :UQ_W]))
        zt.append(lax.dot_general(w_kvt_ref[...], h, (((1,), (1,)), ((), ())),
                                  preferred_element_type=F32))
    halo, kt_c, vt_c = u_halo[...], kt_carry[...], vt_carry[...]
    for n, rows in enumerate(subs):
        zn = z[n] + b_uq_ref[:, :UQ_W]
        u = zn[:, :POOL_W]
        q_ref[rows, :] = (zn[:, POOL_W:] * Q_SCALE).astype(BF16)
        pooled_ref[rows, :] = _pooled(halo, u).astype(BF16)
        halo = u[INPROJ_SUB - POOL_HALO:]
        ztn = zt[n] + b_kvt_ref[...]
        kt, vt = ztn[:ATTN_W], ztn[ATTN_W:]
        for c in range(INPROJ_SUB // KEY_BLOCK):
            cols = slice(c * KEY_BLOCK, (c + 1) * KEY_BLOCK)
            blk = rows.start // KEY_BLOCK + c
            ktb_ref[0, blk] = kt[:, cols].astype(BF16)
            vtb_ref[0, blk] = vt[:, cols].astype(BF16)
        out_cols = pl.ds(pl.multiple_of(j * tm + rows.start, LANES), INPROJ_SUB)
        kt_ref[0, :, out_cols], kt_c = shifted(kt_c, kt)
        vt_ref[0, :, out_cols], vt_c = shifted(vt_c, vt)
    u_halo[...] = halo
    u_tail_ref[0] = halo
    kt_carry[...] = kt_c
    vt_carry[...] = vt_c

    @pl.when(j == pl.num_programs(1) - 1)
    def _():
        kt_ref[0, :, t:] = kt_c[:, :N_META]
        vt_ref[0, :, t:] = vt_c[:, :N_META]


def _inproj_t(x, ln_g, ln_b, w_uq, b_uq, w_kvt, b_kvt, u_meta, kt_meta, vt_meta, streams, tm):
    rows = x.shape[0]
    t = rows // streams
    assert rows == streams * t and t % tm == 0 and POOL_HALO == N_META
    assert tm % INPROJ_SUB == 0 and INPROJ_SUB % KEY_BLOCK == 0
    per = t // tm
    kb = tm // KEY_BLOCK
    row_spec = lambda w: pl.BlockSpec((tm, w), lambda s, j: (s * per + j, 0))
    t_spec = pl.BlockSpec((1, ATTN_W, N_META + t), lambda s, j: (s, 0, 0))
    tb_spec = pl.BlockSpec((1, kb, ATTN_W, KEY_BLOCK), lambda s, j: (s, j, 0, 0))
    out_shape = (
        jax.ShapeDtypeStruct((rows, POOL_W), BF16),
        jax.ShapeDtypeStruct((rows, ATTN_W), BF16),
        jax.ShapeDtypeStruct((streams, POOL_HALO, POOL_W), F32),
        jax.ShapeDtypeStruct((streams, ATTN_W, N_META + t), F32),
        jax.ShapeDtypeStruct((streams, ATTN_W, N_META + t), F32),
        jax.ShapeDtypeStruct((streams, t // KEY_BLOCK, ATTN_W, KEY_BLOCK), BF16),
        jax.ShapeDtypeStruct((streams, t // KEY_BLOCK, ATTN_W, KEY_BLOCK), BF16),
    )
    return pl.pallas_call(
        _inproj_t_kernel,
        grid=(streams, per),
        in_specs=[row_spec(D_MODEL)] + [_whole()] * 9,
        out_specs=(row_spec(POOL_W), row_spec(ATTN_W),
                   pl.BlockSpec((1, POOL_HALO, POOL_W), lambda s, j: (s, 0, 0)),
                   t_spec, t_spec, tb_spec, tb_spec),
        out_shape=out_shape,
        scratch_shapes=[pltpu.VMEM((POOL_HALO, POOL_W), F32),
                        pltpu.VMEM((ATTN_W, LANES), F32), pltpu.VMEM((ATTN_W, LANES), F32)],
        name="inproj_t",
        compiler_params=pltpu.CompilerParams(
            dimension_semantics=("arbitrary", "arbitrary"), vmem_limit_bytes=VMEM_LIMIT_BYTES),
    )(x, ln_g, ln_b, w_uq, b_uq, w_kvt, b_kvt, u_meta, kt_meta, vt_meta)


def _attn_kernel(q_ref, kn_ref, vn_ref, kh_ref, vh_ref, su_ref, o_ref, *scratch,
                 tq, single_q_block, new_keys_t, n_hist_blocks, hist_valid, hist_dma):
    stream = pl.program_id(0)
    i = pl.program_id(1)

    n_lead_hist = min(HIST_LEAD, n_hist_blocks) if hist_dma else 0

    def hist_copies(of_stream, jj, slot):
        hbuf, hsem = scratch[2:]
        start = pl.multiple_of((n_hist_blocks - 1 - jj) * KEY_BLOCK, KEY_BLOCK)
        return [pltpu.make_async_copy(src.at[of_stream, :, pl.ds(start, KEY_BLOCK)],
                                      hbuf.at[slot, which], hsem.at[slot, which])
                for which, src in enumerate((kh_ref, vh_ref))]

    def lead_slot(of_stream, jj):
        return lax.rem(of_stream, 2) * n_lead_hist + jj

    def lead_hist_copies(of_stream):
        return [copy for jj in range(n_lead_hist)
                for copy in hist_copies(of_stream, jj, lead_slot(of_stream, jj))]

    if hist_dma:
        assert single_q_block

        @pl.when(stream == 0)
        def _():
            for copy in lead_hist_copies(stream):
                copy.start()

        for copy in lead_hist_copies(stream):
            copy.wait()

        @pl.when(stream + 1 < pl.num_programs(0))
        def _():
            for copy in lead_hist_copies(stream + 1):
                copy.start()

    lane = lax.broadcasted_iota(jnp.int32, (tq, LANES), 1)
    even = lane < SB_HEAD_DIM
    qs = []
    for p in range(HEAD_PAIRS):
        q = q_ref[0, :, p * LANES:(p + 1) * LANES]
        zero = jnp.zeros_like(q)
        qs.append(jnp.concatenate([jnp.where(even, q, zero), jnp.where(even, zero, q)], axis=0))
    su = su_ref[...]

    row = lax.broadcasted_iota(jnp.int32, (2 * tq, KEY_BLOCK), 0)
    col = lax.broadcasted_iota(jnp.int32, (2 * tq, KEY_BLOCK), 1)
    q_pos = jnp.where(row >= tq, row - tq, row) + i * tq

    pairs = range(HEAD_PAIRS)

    nt_dims = (((1,), (1,)), ((), ()))

    c_ref, acc_ref = scratch[:2]

    def step(blocks, first=False):
        sl = [slice(p * LANES, (p + 1) * LANES) for p in pairs]
        chains = [(b, p) for b in range(len(blocks)) for p in pairs]
        s, tl, carry = {}, {}, {}

        def scores(b, p):
            k_blk, _, _, transposed = blocks[b]
            if transposed:
                s[b, p] = _dot(qs[p], k_blk[sl[p], :])
            else:
                s[b, p] = lax.dot_general(qs[p], k_blk[:, sl[p]], nt_dims,
                                          preferred_element_type=F32)

        def cumulate(b, p):
            vis = blocks[b][2]
            z2 = s[b, p]
            sp = jnp.maximum(z2, 0.0) + jnp.log(1.0 + jnp.exp2(-jnp.abs(z2))) * LOG2_E
            if vis is not None:
                sp = jnp.where(vis, sp, 0.0)
            hi = sp.astype(BF16)
            lo = (sp - hi.astype(F32)).astype(BF16)
            tl[b, p] = _dot(jnp.concatenate([hi, lo], axis=1), su)

        def weigh(b, p):
            _, v_blk, vis, transposed = blocks[b]
            if b > 0:
                c, acc = carry[p]
            elif first:
                c = acc = jnp.zeros((2 * tq, LANES), F32)
            else:
                c, acc = c_ref[p], acc_ref[p]
            w = jnp.exp2(s[b, p] + tl[b, p][:, :KEY_BLOCK] + c)
            if vis is not None:
                w = jnp.where(vis, w, 0.0)
            wb = w.astype(BF16)
            if transposed:
                pv = lax.dot_general(wb, v_blk[sl[p], :], nt_dims, preferred_element_type=F32)
            else:
                pv = _dot(wb, v_blk[:, sl[p]])
            carry[p] = (c + tl[b, p][:, KEY_BLOCK:], acc + pv)
            if b == len(blocks) - 1:
                c_ref[p], acc_ref[p] = carry[p]

        n = len(chains)
        skew = n if 2 * tq < SKEW_MIN_ROWS else STAGE_SKEW
        for t in range(n + 2 * skew):
            if t < n:
                scores(*chains[t])
            if 0 <= t - skew < n:
                cumulate(*chains[t - skew])
            if 0 <= t - 2 * skew < n:
                weigh(*chains[t - 2 * skew])
        c_max = functools.reduce(jnp.maximum, [carry[p][0] for p in pairs])
        return (jnp.max(c_max) > DEAD_LOG2_WEIGHT).astype(jnp.int32)

    def new_block(j, vis):
        if new_keys_t:
            return kn_ref[0, j], vn_ref[0, j], vis, True
        start = pl.multiple_of(j * KEY_BLOCK, KEY_BLOCK)
        return (kn_ref[0, pl.ds(start, KEY_BLOCK), :], vn_ref[0, pl.ds(start, KEY_BLOCK), :], vis,
                False)

    def hist_block(slot):
        hbuf = scratch[2]
        return hbuf[slot, 0].astype(BF16), hbuf[slot, 1].astype(BF16), None, True

    def new_step(js):
        return step([new_block(j, None) for j in js])

    n_diag = max(tq // KEY_BLOCK, 1)
    if single_q_block:
        first_diag = 0
    else:
        first_diag = i * (tq // KEY_BLOCK) if tq >= KEY_BLOCK else (i * tq) // KEY_BLOCK

    def lead(extra):
        blocks = [new_block(first_diag + d, (first_diag + d) * KEY_BLOCK + col < q_pos)
                  for d in reversed(range(n_diag))]
        blocks += [new_block(first_diag - 1 - e, None) for e in range(extra)]
        if single_q_block:
            blocks += [hist_block(lead_slot(stream, jj)) for jj in range(n_lead_hist)]
        return step(blocks, first=True)

    if single_q_block:
        live, n_rest = lead(0), 0
    else:
        has_extra = first_diag >= LEAD_EXTRA
        live = lax.cond(has_extra, lambda: lead(LEAD_EXTRA), lambda: lead(0))
        n_rest = first_diag - jnp.where(has_extra, LEAD_EXTRA, 0)

    def visit(n_blocks, body, live):
        def cond(state):
            return jnp.logical_and(state[0] < n_blocks, state[1] > 0)

        return lax.while_loop(cond, lambda state: (state[0] + 1, body(state[0])),
                              (jnp.int32(0), live))[1]

    if not single_q_block:
        live = visit(lax.shift_right_logical(n_rest, 1),
                     lambda jj: new_step([n_rest - 1 - 2 * jj, n_rest - 2 - 2 * jj]), live)
        live = visit(jnp.bitwise_and(n_rest, 1), lambda jj: new_step([0]), live)

    partial = hist_valid % KEY_BLOCK

    if hist_dma:
        assert not partial

        def hist_body(jj):
            copies = hist_copies(stream, n_lead_hist + jj, HIST_SLOTS - 1)
            for copy in copies:
                copy.start()
            for copy in copies:
                copy.wait()
            return step([hist_block(HIST_SLOTS - 1)])

        visit(n_hist_blocks - n_lead_hist, hist_body, live)
    else:
        assert n_hist_blocks == 1
        vis = col < partial if partial else None
        visit(1, lambda jj: step([(kh_ref[0].astype(BF16), vh_ref[0].astype(BF16), vis, True)]),
              live)

    for p in range(HEAD_PAIRS):
        acc = acc_ref[p]
        o_ref[0, :, p * LANES:(p + 1) * LANES] = (
            jnp.where(even, acc[:tq], acc[tq:]).astype(o_ref.dtype))


def _cumsum_matrix():
    r = lax.broadcasted_iota(jnp.int32, (2 * KEY_BLOCK, KEY_BLOCK + LANES), 0) % KEY_BLOCK
    c = lax.broadcasted_iota(jnp.int32, (2 * KEY_BLOCK, KEY_BLOCK + LANES), 1)
    return jnp.where((c >= KEY_BLOCK) | (r >= c), -1.0, 0.0).astype(BF16)


def _attention(q, k_new, v_new, k_hist, v_hist, *, new_keys_t, hist_valid, hist_dma, tq):
    b, t, _ = q.shape
    assert t % tq == 0
    if new_keys_t:
        assert k_new.shape[1] * KEY_BLOCK == t
        new_spec = pl.BlockSpec((1,) + k_new.shape[1:], lambda bi, i: (bi, 0, 0, 0))
    else:
        assert k_new.shape[1] % KEY_BLOCK == 0 and k_new.shape[1] >= t
        new_spec = pl.BlockSpec((1,) + k_new.shape[1:], lambda bi, i: (bi, 0, 0))
    hl = k_hist.shape[2]
    assert hl % KEY_BLOCK == 0 and hist_valid <= hl
    scratch = [pltpu.VMEM((HEAD_PAIRS, 2 * tq, LANES), F32)] * 2
    if hist_dma:
        hist_spec = pl.BlockSpec(memory_space=pl.ANY)
        scratch += [pltpu.VMEM((HIST_SLOTS, 2, ATTN_W, KEY_BLOCK), k_hist.dtype),
                    pltpu.SemaphoreType.DMA((HIST_SLOTS, 2))]
    else:
        hist_spec = pl.BlockSpec((1, ATTN_W, KEY_BLOCK), lambda bi, i: (0, 0, 0))
    kern = functools.partial(_attn_kernel, tq=tq, single_q_block=(t == tq), new_keys_t=new_keys_t,
                             n_hist_blocks=hl // KEY_BLOCK, hist_valid=hist_valid,
                             hist_dma=hist_dma)
    return pl.pallas_call(
        kern,
        grid=(b, t // tq),
        in_specs=[pl.BlockSpec((1, tq, ATTN_W), lambda bi, i: (bi, i, 0)),
                  new_spec, new_spec, hist_spec, hist_spec, _whole()],
        out_specs=pl.BlockSpec((1, tq, ATTN_W), lambda bi, i: (bi, i, 0)),
        out_shape=jax.ShapeDtypeStruct((b, t, ATTN_W), BF16),
        scratch_shapes=scratch,
        name="attn",
        compiler_params=pltpu.CompilerParams(
            dimension_semantics=("arbitrary", "arbitrary"),
            vmem_limit_bytes=VMEM_LIMIT_BYTES),
    )(q, k_new, v_new, k_hist, v_hist, _cumsum_matrix())


def _post_kernel(*refs, first_tiles):
    (x1, pooled1, attn1, x2, pooled2, attn2), weights, (y1, y2) = refs[:6], refs[6:-2], refs[-2:]
    step = pl.program_id(0)

    @pl.when(step < first_tiles)
    def _():
        _post_tile(x1, pooled1, attn1, *weights, y1)

    @pl.when(step >= first_tiles)
    def _():
        _post_tile(x2, pooled2, attn2, *weights, y2)


def _post_tile(x_ref, pooled_ref, attn_ref, lng, lnb, wg, bg, wpg, pscale, wbp, wba, wo, bo,
               l1g, l1b, w1, b1, w2, b2, l2g, l2b, y_ref):
    subs = [slice(r, r + POST_SUB) for r in range(0, x_ref.shape[0], POST_SUB)]
    groups = range(len(POOL_WINDOWS))
    h = [_layer_norm(x_ref[rows, :], lng[...], lnb[...]) for rows in subs]
    gates = [_dot(hn.astype(BF16), wg[...]) + bg[:, UQKV_W:] for hn in h]
    pool_out = []
    for rows in subs:
        pooled = pooled_ref[rows, :]
        pool_out.append(jnp.concatenate(
            [_dot(pooled[:, g * POOL_GC:(g + 1) * POOL_GC], wpg[g]) for g in groups],
            axis=-1) * pscale[...])
    branch_pool = [_dot(po.astype(BF16), wbp[...]) for po in pool_out]
    branch_attn = [_dot(attn_ref[rows, :], wba[...]) for rows in subs]
    mix = [jax.nn.sigmoid(g[:, :D_MODEL]) * bp + jax.nn.sigmoid(g[:, D_MODEL:]) * ba
           for g, bp, ba in zip(gates, branch_pool, branch_attn)]
    res = [_dot(m.astype(BF16), wo[...]) + bo[...] for m in mix]
    h1 = [_layer_norm(ALPHA * hn + r, l1g[...], l1b[...]) for hn, r in zip(h, res)]
    hid = [jnp.square(jnp.maximum(_dot(hn.astype(BF16), w1[...]) + b1[...], 0.0)) for hn in h1]
    f = [_dot(hd.astype(BF16), w2[...]) + b2[...] for hd in hid]
    for rows, hn, fn in zip(subs, h1, f):
        y_ref[rows, :] = _layer_norm(ALPHA * hn + fn, l2g[...], l2b[...])


def _post(first, second, weights, tm):
    tm1, tm2 = tm
    n1, n2 = first[0].shape[0] // tm1, second[0].shape[0] // tm2
    assert first[0].shape[0] == n1 * tm1 and second[0].shape[0] == n2 * tm2
    assert tm1 % POST_SUB == 0 and tm2 % POST_SUB == 0
    spec1 = lambda w: pl.BlockSpec((tm1, w), lambda i: (jnp.minimum(i, n1 - 1), 0))
    spec2 = lambda w: pl.BlockSpec((tm2, w), lambda i: (jnp.maximum(i - n1, 0), 0))
    widths = (D_MODEL, POOL_W, ATTN_W)
    return pl.pallas_call(
        functools.partial(_post_kernel, first_tiles=n1),
        grid=(n1 + n2,),
        in_specs=[spec1(w) for w in widths] + [spec2(w) for w in widths]
                 + [_whole() for _ in weights],
        out_specs=(spec1(D_MODEL), spec2(D_MODEL)),
        out_shape=(jax.ShapeDtypeStruct((n1 * tm1, D_MODEL), F32),
                   jax.ShapeDtypeStruct((n2 * tm2, D_MODEL), F32)),
        name="post",
        compiler_params=pltpu.CompilerParams(
            dimension_semantics=("arbitrary",), vmem_limit_bytes=VMEM_LIMIT_BYTES),
    )(*first, *second, *weights)


def _row(v):
    return v.reshape(1, -1).astype(F32)


def _last_rows(halo, u):
    if u.shape[1] >= POOL_STATE:
        return u[None, :, -POOL_STATE:]
    return jnp.concatenate([halo, u], axis=1)[None, :, -POOL_STATE:]


def kernel(x_prompt, x_sample, cache_k, cache_v, state_pool, meta, ln_in_g, ln_in_b, w_in, b_in,
           w_pool_grp, pool_scale, w_br_pool, w_br_attn, w_out, b_out, ln1_g, ln1_b,
           w_mlp1, b_mlp1, w_mlp2, b_mlp2, ln2_g, ln2_b):
    assert w_in.shape[0] == DEPTH
    bp, seq, _ = x_prompt.shape
    bs, dec_seq, _ = x_sample.shape
    past = cache_k.shape[2]

    lng, lnb = _row(ln_in_g), _row(ln_in_b)
    w_uqkv = w_in[0, :, :UQKV_W].astype(BF16)
    b_in_row = _row(b_in[0])
    post_weights = (
        lng, lnb, w_in[0, :, UQKV_W:].astype(BF16), b_in_row, w_pool_grp[0].astype(BF16),
        _row(pool_scale[0]),
        w_br_pool[0].astype(BF16), w_br_attn[0].astype(BF16), w_out[0].astype(BF16), _row(b_out[0]),
        _row(ln1_g[0]), _row(ln1_b[0]), w_mlp1[0].astype(BF16), _row(b_mlp1[0]),
        w_mlp2[0].astype(BF16), _row(b_mlp2[0]), _row(ln2_g[0]), _row(ln2_b[0]))

    inproj = functools.partial(_inproj, ln_g=lng, ln_b=lnb, w_uqkv=w_uqkv, b_in_row=b_in_row)
    xp = x_prompt.reshape(bp * seq, D_MODEL)
    xs = x_sample.reshape(bs * dec_seq, D_MODEL)
    u_m, _, _, kf_m, vf_m, kb_m, vb_m = inproj(
        meta.astype(F32), jnp.zeros((1, POOL_HALO, POOL_W), F32), tm=N_META)
    meta_t = lambda rows: jnp.pad(rows.T, ((0, 0), (0, KEY_BLOCK - N_META)))
    pooled_p, q_p, u_tail_p, kt_p, vt_p, ktb_p, vtb_p = _inproj_t(
        xp, lng, lnb, w_uqkv, b_in_row,
        w_uqkv[:, UQ_W:].T, b_in[0, UQ_W:UQKV_W].reshape(-1, 1).astype(F32),
        u_m, meta_t(kf_m), meta_t(vf_m), streams=bp, tm=512)
    halo_s = jnp.pad(state_pool[0], ((0, 0), (POOL_HALO - POOL_STATE, 0), (0, 0)))
    u_s, pooled_s, q_s, kf_s, vf_s, kb_s, vb_s = inproj(xs, halo_s, tm=512)

    attn_p = _attention(
        q_p.reshape(bp, seq, ATTN_W), ktb_p, vtb_p, kb_m[0].T[None], vb_m[0].T[None],
        new_keys_t=True, hist_valid=N_META, hist_dma=False, tq=128)

    u_s3 = u_s.reshape(bs, dec_seq, POOL_W)
    keys_on_lanes = lambda c: c.transpose(0, 2, 3, 1).reshape(bs, ATTN_W, past)
    attn_s = _attention(
        q_s.reshape(bs, dec_seq, ATTN_W), kb_s, vb_s,
        keys_on_lanes(cache_k[0]), keys_on_lanes(cache_v[0]),
        new_keys_t=False, hist_valid=past, hist_dma=True, tq=dec_seq)

    y_p, y_s = _post(
        (xp, pooled_p, attn_p.reshape(bp * seq, ATTN_W)),
        (xs, pooled_s, attn_s.reshape(bs * dec_seq, ATTN_W)),
        post_weights, tm=(512, 256))

    def seq_major(kv_t):
        full = kv_t.reshape(DEPTH, bp, SB_HEADS, SB_HEAD_DIM, N_META + seq)
        return full.transpose(0, 1, 4, 2, 3)

    assert seq >= POOL_STATE
    heads_s = (DEPTH, bs, dec_seq, SB_HEADS, SB_HEAD_DIM)
    return (
        y_p.reshape(bp, seq, D_MODEL),
        y_s.reshape(bs, dec_seq, D_MODEL),
        seq_major(kt_p),
        seq_major(vt_p),
        u_tail_p[None, :, POOL_HALO - POOL_STATE:],
        kf_s.reshape(heads_s),
        vf_s.reshape(heads_s),
        _last_rows(halo_s, u_s3),
    )
```

```python
import functools

import jax
import jax.numpy as jnp
from jax import lax
from jax.experimental import pallas as pl
from jax.experimental.pallas import tpu as pltpu

D_MODEL = 1024
N_META = 16
POOL_W = D_MODEL // 2
POOL_WINDOWS = (2, 4, 8, 16)
POOL_GC = POOL_W // len(POOL_WINDOWS)
POOL_STATE = max(POOL_WINDOWS) - 1
POOL_HALO = POOL_STATE + 1
SB_HEADS = 8
SB_HEAD_DIM = 64
ATTN_W = SB_HEADS * SB_HEAD_DIM
D_FF = 4 * D_MODEL
UQ_W = POOL_W + ATTN_W
UQKV_W = POOL_W + 3 * ATTN_W
LN_EPS = 1e-5
DEPTH = 1
ALPHA = (2.0 * DEPTH) ** 0.25
SB_SCALE = SB_HEAD_DIM ** -0.5
LOG2_E = 1.4426950408889634
Q_SCALE = SB_SCALE * LOG2_E

LANES = 128
KEY_BLOCK = LANES
HEAD_PAIRS = ATTN_W // LANES
STAGE_SKEW = 2
SKEW_MIN_ROWS = 256
HIST_LEAD = 2
HIST_SLOTS = 2 * HIST_LEAD + 1
ATTN_Q_PER_STEP = 2
LEAD_EXTRA = 2
POST_SUB = 256
INPROJ_SUB = 256
VMEM_LIMIT_BYTES = 56 * 1024 * 1024
DEAD_LOG2_WEIGHT = -160.0

F32 = jnp.float32
BF16 = jnp.bfloat16


def _layer_norm(x, g, b):
    mu = jnp.mean(x, axis=-1, keepdims=True)
    xc = x - mu
    var = jnp.mean(xc * xc, axis=-1, keepdims=True)
    return xc * lax.rsqrt(var + LN_EPS) * g + b


def _dot(a, b):
    return jnp.dot(a, b, preferred_element_type=F32)


def _whole():
    return pl.BlockSpec(memory_space=pltpu.VMEM)


def _pooled(halo, u):
    full = jnp.concatenate([halo, u], axis=0)
    outs = []
    for g, w in enumerate(POOL_WINDOWS):
        x = full[:, g * POOL_GC:(g + 1) * POOL_GC]
        s = x
        span = 1
        while span < w:
            s = s + pltpu.roll(s, span, axis=0)
            span *= 2
        outs.append(s[POOL_HALO:] * (1.0 / w) - x[POOL_HALO:])
    return jnp.concatenate(outs, axis=-1)


def _inproj_kernel(x_ref, halo_ref, g_ref, b_ref, w_ref, bias_ref,
                   u_ref, pooled_ref, q_ref, kf_ref, vf_ref, kb_ref, vb_ref):
    h = _layer_norm(x_ref[...], g_ref[...], b_ref[...])
    z = _dot(h.astype(BF16), w_ref[...]) + bias_ref[:, :UQKV_W]
    u = z[:, :POOL_W]
    u_ref[...] = u
    q_ref[...] = (z[:, POOL_W:POOL_W + ATTN_W] * Q_SCALE).astype(BF16)
    k = z[:, POOL_W + ATTN_W:POOL_W + 2 * ATTN_W]
    v = z[:, POOL_W + 2 * ATTN_W:]
    kf_ref[...] = k
    vf_ref[...] = v
    t = x_ref.shape[0] // halo_ref.shape[0]
    pad = jnp.zeros((kb_ref.shape[1] - t, ATTN_W), BF16)
    for s in range(halo_ref.shape[0]):
        rows = slice(s * t, (s + 1) * t)
        pooled_ref[rows, :] = _pooled(halo_ref[s], u[rows]).astype(BF16)
        kb_ref[s] = jnp.concatenate([k[rows].astype(BF16), pad], axis=0)
        vb_ref[s] = jnp.concatenate([v[rows].astype(BF16), pad], axis=0)


def _inproj(x, halo, ln_g, ln_b, w_uqkv, b_in_row, tm):
    rows = x.shape[0]
    streams = halo.shape[0]
    t = rows // streams
    tn = -(-t // KEY_BLOCK) * KEY_BLOCK
    assert rows % tm == 0 and tm % t == 0 and rows == t * streams
    row_spec = lambda w: pl.BlockSpec((tm, w), lambda i: (i, 0))
    stream_spec = lambda r, w: pl.BlockSpec((tm // t, r, w), lambda i: (i, 0, 0))
    out_shape = (
        jax.ShapeDtypeStruct((rows, POOL_W), F32),
        jax.ShapeDtypeStruct((rows, POOL_W), BF16),
        jax.ShapeDtypeStruct((rows, ATTN_W), BF16),
        jax.ShapeDtypeStruct((rows, ATTN_W), F32),
        jax.ShapeDtypeStruct((rows, ATTN_W), F32),
        jax.ShapeDtypeStruct((streams, tn, ATTN_W), BF16),
        jax.ShapeDtypeStruct((streams, tn, ATTN_W), BF16),
    )
    return pl.pallas_call(
        _inproj_kernel,
        grid=(rows // tm,),
        in_specs=[row_spec(D_MODEL), stream_spec(POOL_HALO, POOL_W),
                  _whole(), _whole(), _whole(), _whole()],
        out_specs=tuple(row_spec(s.shape[1]) for s in out_shape[:5])
                  + (stream_spec(tn, ATTN_W), stream_spec(tn, ATTN_W)),
        out_shape=out_shape,
        name="inproj",
        compiler_params=pltpu.CompilerParams(
            dimension_semantics=("arbitrary",), vmem_limit_bytes=VMEM_LIMIT_BYTES),
    )(x, halo, ln_g, ln_b, w_uqkv, b_in_row)


def _inproj_t_kernel(x_ref, g_ref, b_ref, w_uq_ref, b_uq_ref, w_kvt_ref, b_kvt_ref,
                     u_meta_ref, kt_meta_ref, vt_meta_ref,
                     pooled_ref, q_ref, u_tail_ref, kt_ref, vt_ref, ktb_ref, vtb_ref,
                     u_halo, kt_carry, vt_carry):
    j = pl.program_id(1)
    tm = x_ref.shape[0]
    t = kt_ref.shape[2] - N_META
    lane = lax.broadcasted_iota(jnp.int32, (ATTN_W, LANES), 1)

    @pl.when(j == 0)
    def _():
        u_halo[...] = u_meta_ref[...]
        kt_carry[...] = kt_meta_ref[...]
        vt_carry[...] = vt_meta_ref[...]

    def shifted(carry, cur):
        rolled = pltpu.roll(cur, N_META, axis=1)
        out = jnp.concatenate(
            [jnp.where(lane < N_META, carry, rolled[:, :LANES]), rolled[:, LANES:]], axis=1)
        return out, rolled[:, :LANES]

    subs = [slice(r, r + INPROJ_SUB) for r in range(0, tm, INPROJ_SUB)]
    hb = [_layer_norm(x_ref[rows, :], g_ref[...], b_ref[...]).astype(BF16) for rows in subs]
    z, zt = [], []
    for h in hb:
        z.append(_dot(h, w_uq_ref[:, :UQ_W]))
        zt.append(lax.dot_general(w_kvt_ref[...], h, (((1,), (1,)), ((), ())),
                                  preferred_element_type=F32))
    halo, kt_c, vt_c = u_halo[...], kt_carry[...], vt_carry[...]
    for n, rows in enumerate(subs):
        zn = z[n] + b_uq_ref[:, :UQ_W]
        u = zn[:, :POOL_W]
        q_ref[rows, :] = (zn[:, POOL_W:] * Q_SCALE).astype(BF16)
        pooled_ref[rows, :] = _pooled(halo, u).astype(BF16)
        halo = u[INPROJ_SUB - POOL_HALO:]
        ztn = zt[n] + b_kvt_ref[...]
        kt, vt = ztn[:ATTN_W], ztn[ATTN_W:]
        for c in range(INPROJ_SUB // KEY_BLOCK):
            cols = slice(c * KEY_BLOCK, (c + 1) * KEY_BLOCK)
            blk = rows.start // KEY_BLOCK + c
            ktb_ref[0, blk] = kt[:, cols].astype(BF16)
            vtb_ref[0, blk] = vt[:, cols].astype(BF16)
        out_cols = pl.ds(pl.multiple_of(j * tm + rows.start, LANES), INPROJ_SUB)
        kt_ref[0, :, out_cols], kt_c = shifted(kt_c, kt)
        vt_ref[0, :, out_cols], vt_c = shifted(vt_c, vt)
    u_halo[...] = halo
    u_tail_ref[0] = halo
    kt_carry[...] = kt_c
    vt_carry[...] = vt_c

    @pl.when(j == pl.num_programs(1) - 1)
    def _():
        kt_ref[0, :, t:] = kt_c[:, :N_META]
        vt_ref[0, :, t:] = vt_c[:, :N_META]


def _inproj_t(x, ln_g, ln_b, w_uq, b_uq, w_kvt, b_kvt, u_meta, kt_meta, vt_meta, streams, tm):
    rows = x.shape[0]
    t = rows // streams
    assert rows == streams * t and t % tm == 0 and POOL_HALO == N_META
    assert tm % INPROJ_SUB == 0 and INPROJ_SUB % KEY_BLOCK == 0
    per = t // tm
    kb = tm // KEY_BLOCK
    row_spec = lambda w: pl.BlockSpec((tm, w), lambda s, j: (s * per + j, 0))
    t_spec = pl.BlockSpec((1, ATTN_W, N_META + t), lambda s, j: (s, 0, 0))
    tb_spec = pl.BlockSpec((1, kb, ATTN_W, KEY_BLOCK), lambda s, j: (s, j, 0, 0))
    out_shape = (
        jax.ShapeDtypeStruct((rows, POOL_W), BF16),
        jax.ShapeDtypeStruct((rows, ATTN_W), BF16),
        jax.ShapeDtypeStruct((streams, POOL_HALO, POOL_W), F32),
        jax.ShapeDtypeStruct((streams, ATTN_W, N_META + t), F32),
        jax.ShapeDtypeStruct((streams, ATTN_W, N_META + t), F32),
        jax.ShapeDtypeStruct((streams, t // KEY_BLOCK, ATTN_W, KEY_BLOCK), BF16),
        jax.ShapeDtypeStruct((streams, t // KEY_BLOCK, ATTN_W, KEY_BLOCK), BF16),
    )
    return pl.pallas_call(
        _inproj_t_kernel,
        grid=(streams, per),
        in_specs=[row_spec(D_MODEL)] + [_whole()] * 9,
        out_specs=(row_spec(POOL_W), row_spec(ATTN_W),
                   pl.BlockSpec((1, POOL_HALO, POOL_W), lambda s, j: (s, 0, 0)),
                   t_spec, t_spec, tb_spec, tb_spec),
        out_shape=out_shape,
        scratch_shapes=[pltpu.VMEM((POOL_HALO, POOL_W), F32),
                        pltpu.VMEM((ATTN_W, LANES), F32), pltpu.VMEM((ATTN_W, LANES), F32)],
        name="inproj_t",
        compiler_params=pltpu.CompilerParams(
            dimension_semantics=("arbitrary", "arbitrary"), vmem_limit_bytes=VMEM_LIMIT_BYTES),
    )(x, ln_g, ln_b, w_uq, b_uq, w_kvt, b_kvt, u_meta, kt_meta, vt_meta)


def _attn_kernel(*refs, q_per_step, **static):
    for sub in range(q_per_step):
        _attn_q_block(*refs, sub=sub, q_per_step=q_per_step, **static)


def _attn_q_block(q_ref, kn_ref, vn_ref, kh_ref, vh_ref, su_ref, o_ref, *scratch, sub, q_per_step,
                  tq, single_q_block, new_keys_t, n_hist_blocks, hist_valid, hist_dma):
    stream = pl.program_id(0)
    i = pl.program_id(1) * q_per_step + sub
    q_rows = slice(sub * tq, (sub + 1) * tq)

    n_lead_hist = min(HIST_LEAD, n_hist_blocks) if hist_dma else 0

    def hist_copies(of_stream, jj, slot):
        hbuf, hsem = scratch[2:]
        start = pl.multiple_of((n_hist_blocks - 1 - jj) * KEY_BLOCK, KEY_BLOCK)
        return [pltpu.make_async_copy(src.at[of_stream, :, pl.ds(start, KEY_BLOCK)],
                                      hbuf.at[slot, which], hsem.at[slot, which])
                for which, src in enumerate((kh_ref, vh_ref))]

    def lead_slot(of_stream, jj):
        return lax.rem(of_stream, 2) * n_lead_hist + jj

    def lead_hist_copies(of_stream):
        return [copy for jj in range(n_lead_hist)
                for copy in hist_copies(of_stream, jj, lead_slot(of_stream, jj))]

    if hist_dma:
        assert single_q_block

        @pl.when(stream == 0)
        def _():
            for copy in lead_hist_copies(stream):
                copy.start()

        for copy in lead_hist_copies(stream):
            copy.wait()

        @pl.when(stream + 1 < pl.num_programs(0))
        def _():
            for copy in lead_hist_copies(stream + 1):
                copy.start()

    lane = lax.broadcasted_iota(jnp.int32, (tq, LANES), 1)
    even = lane < SB_HEAD_DIM
    qs = []
    for p in range(HEAD_PAIRS):
        q = q_ref[0, q_rows, p * LANES:(p + 1) * LANES]
        zero = jnp.zeros_like(q)
        qs.append(jnp.concatenate([jnp.where(even, q, zero), jnp.where(even, zero, q)], axis=0))
    su = su_ref[...]

    row = lax.broadcasted_iota(jnp.int32, (2 * tq, KEY_BLOCK), 0)
    col = lax.broadcasted_iota(jnp.int32, (2 * tq, KEY_BLOCK), 1)
    q_pos = jnp.where(row >= tq, row - tq, row) + i * tq

    pairs = range(HEAD_PAIRS)

    nt_dims = (((1,), (1,)), ((), ()))

    c_ref, acc_ref = scratch[:2]

    def step(blocks, first=False):
        sl = [slice(p * LANES, (p + 1) * LANES) for p in pairs]
        chains = [(b, p) for b in range(len(blocks)) for p in pairs]
        s, tl, carry = {}, {}, {}

        def scores(b, p):
            k_blk, _, _, transposed = blocks[b]
            if transposed:
                s[b, p] = _dot(qs[p], k_blk[sl[p], :])
            else:
                s[b, p] = lax.dot_general(qs[p], k_blk[:, sl[p]], nt_dims,
                                          preferred_element_type=F32)

        def cumulate(b, p):
            vis = blocks[b][2]
            z2 = s[b, p]
            sp = jnp.maximum(z2, 0.0) + jnp.log(1.0 + jnp.exp2(-jnp.abs(z2))) * LOG2_E
            if vis is not None:
                sp = jnp.where(vis, sp, 0.0)
            hi = sp.astype(BF16)
            lo = (sp - hi.astype(F32)).astype(BF16)
            tl[b, p] = _dot(jnp.concatenate([hi, lo], axis=1), su)

        def weigh(b, p):
            _, v_blk, vis, transposed = blocks[b]
            if b > 0:
                c, acc = carry[p]
            elif first:
                c = acc = jnp.zeros((2 * tq, LANES), F32)
            else:
                c, acc = c_ref[p], acc_ref[p]
            w = jnp.exp2(s[b, p] + tl[b, p][:, :KEY_BLOCK] + c)
            if vis is not None:
                w = jnp.where(vis, w, 0.0)
            wb = w.astype(BF16)
            if transposed:
                pv = lax.dot_general(wb, v_blk[sl[p], :], nt_dims, preferred_element_type=F32)
            else:
                pv = _dot(wb, v_blk[:, sl[p]])
            carry[p] = (c + tl[b, p][:, KEY_BLOCK:], acc + pv)
            if b == len(blocks) - 1:
                c_ref[p], acc_ref[p] = carry[p]

        n = len(chains)
        skew = n if 2 * tq < SKEW_MIN_ROWS else STAGE_SKEW
        for t in range(n + 2 * skew):
            if t < n:
                scores(*chains[t])
            if 0 <= t - skew < n:
                cumulate(*chains[t - skew])
            if 0 <= t - 2 * skew < n:
                weigh(*chains[t - 2 * skew])
        c_max = functools.reduce(jnp.maximum, [carry[p][0] for p in pairs])
        return (jnp.max(c_max) > DEAD_LOG2_WEIGHT).astype(jnp.int32)

    def new_block(j, vis):
        if new_keys_t:
            return kn_ref[0, j], vn_ref[0, j], vis, True
        start = pl.multiple_of(j * KEY_BLOCK, KEY_BLOCK)
        return (kn_ref[0, pl.ds(start, KEY_BLOCK), :], vn_ref[0, pl.ds(start, KEY_BLOCK), :], vis,
                False)

    def hist_block(slot):
        hbuf = scratch[2]
        return hbuf[slot, 0].astype(BF16), hbuf[slot, 1].astype(BF16), None, True

    def new_step(js):
        return step([new_block(j, None) for j in js])

    n_diag = max(tq // KEY_BLOCK, 1)
    if single_q_block:
        first_diag = 0
    else:
        first_diag = i * (tq // KEY_BLOCK) if tq >= KEY_BLOCK else (i * tq) // KEY_BLOCK

    def lead(extra):
        blocks = [new_block(first_diag + d, (first_diag + d) * KEY_BLOCK + col < q_pos)
                  for d in reversed(range(n_diag))]
        blocks += [new_block(first_diag - 1 - e, None) for e in range(extra)]
        if single_q_block:
            blocks += [hist_block(lead_slot(stream, jj)) for jj in range(n_lead_hist)]
        return step(blocks, first=True)

    if single_q_block:
        live, n_rest = lead(0), 0
    else:
        has_extra = first_diag >= LEAD_EXTRA
        live = lax.cond(has_extra, lambda: lead(LEAD_EXTRA), lambda: lead(0))
        n_rest = first_diag - jnp.where(has_extra, LEAD_EXTRA, 0)

    def visit(n_blocks, body, live):
        def cond(state):
            return jnp.logical_and(state[0] < n_blocks, state[1] > 0)

        return lax.while_loop(cond, lambda state: (state[0] + 1, body(state[0])),
                              (jnp.int32(0), live))[1]

    if not single_q_block:
        live = visit(lax.shift_right_logical(n_rest, 1),
                     lambda jj: new_step([n_rest - 1 - 2 * jj, n_rest - 2 - 2 * jj]), live)
        live = visit(jnp.bitwise_and(n_rest, 1), lambda jj: new_step([0]), live)

    partial = hist_valid % KEY_BLOCK

    if hist_dma:
        assert not partial

        def hist_body(jj):
            copies = hist_copies(stream, n_lead_hist + jj, HIST_SLOTS - 1)
            for copy in copies:
                copy.start()
            for copy in copies:
                copy.wait()
            return step([hist_block(HIST_SLOTS - 1)])

        visit(n_hist_blocks - n_lead_hist, hist_body, live)
    else:
        assert n_hist_blocks == 1
        vis = col < partial if partial else None
        visit(1, lambda jj: step([(kh_ref[0].astype(BF16), vh_ref[0].astype(BF16), vis, True)]),
              live)

    for p in range(HEAD_PAIRS):
        acc = acc_ref[p]
        o_ref[0, q_rows, p * LANES:(p + 1) * LANES] = (
            jnp.where(even, acc[:tq], acc[tq:]).astype(o_ref.dtype))


def _cumsum_matrix():
    r = lax.broadcasted_iota(jnp.int32, (2 * KEY_BLOCK, KEY_BLOCK + LANES), 0) % KEY_BLOCK
    c = lax.broadcasted_iota(jnp.int32, (2 * KEY_BLOCK, KEY_BLOCK + LANES), 1)
    return jnp.where((c >= KEY_BLOCK) | (r >= c), -1.0, 0.0).astype(BF16)


def _attention(q, k_new, v_new, k_hist, v_hist, *, new_keys_t, hist_valid, hist_dma, tq):
    b, t, _ = q.shape
    q_per_step = ATTN_Q_PER_STEP if t % (tq * ATTN_Q_PER_STEP) == 0 else 1
    tstep = tq * q_per_step
    assert t % tstep == 0
    if new_keys_t:
        assert k_new.shape[1] * KEY_BLOCK == t
        new_spec = pl.BlockSpec((1,) + k_new.shape[1:], lambda bi, i: (bi, 0, 0, 0))
    else:
        assert k_new.shape[1] % KEY_BLOCK == 0 and k_new.shape[1] >= t
        new_spec = pl.BlockSpec((1,) + k_new.shape[1:], lambda bi, i: (bi, 0, 0))
    hl = k_hist.shape[2]
    assert hl % KEY_BLOCK == 0 and hist_valid <= hl
    scratch = [pltpu.VMEM((HEAD_PAIRS, 2 * tq, LANES), F32)] * 2
    if hist_dma:
        hist_spec = pl.BlockSpec(memory_space=pl.ANY)
        scratch += [pltpu.VMEM((HIST_SLOTS, 2, ATTN_W, KEY_BLOCK), k_hist.dtype),
                    pltpu.SemaphoreType.DMA((HIST_SLOTS, 2))]
    else:
        hist_spec = pl.BlockSpec((1, ATTN_W, KEY_BLOCK), lambda bi, i: (0, 0, 0))
    kern = functools.partial(_attn_kernel, q_per_step=q_per_step, tq=tq, single_q_block=(t == tq),
                             new_keys_t=new_keys_t, n_hist_blocks=hl // KEY_BLOCK,
                             hist_valid=hist_valid, hist_dma=hist_dma)
    return pl.pallas_call(
        kern,
        grid=(b, t // tstep),
        in_specs=[pl.BlockSpec((1, tstep, ATTN_W), lambda bi, i: (bi, i, 0)),
                  new_spec, new_spec, hist_spec, hist_spec, _whole()],
        out_specs=pl.BlockSpec((1, tstep, ATTN_W), lambda bi, i: (bi, i, 0)),
        out_shape=jax.ShapeDtypeStruct((b, t, ATTN_W), BF16),
        scratch_shapes=scratch,
        name="attn",
        compiler_params=pltpu.CompilerParams(
            dimension_semantics=("arbitrary", "arbitrary"),
            vmem_limit_bytes=VMEM_LIMIT_BYTES),
    )(q, k_new, v_new, k_hist, v_hist, _cumsum_matrix())


def _post_kernel(*refs, first_tiles):
    (x1, pooled1, attn1, x2, pooled2, attn2), weights, (y1, y2) = refs[:6], refs[6:-2], refs[-2:]
    step = pl.program_id(0)

    @pl.when(step < first_tiles)
    def _():
        _post_tile(x1, pooled1, attn1, *weights, y1)

    @pl.when(step >= first_tiles)
    def _():
        _post_tile(x2, pooled2, attn2, *weights, y2)


def _post_tile(x_ref, pooled_ref, attn_ref, lng, lnb, wg, bg, wpg, pscale, wbp, wba, wo, bo,
               l1g, l1b, w1, b1, w2, b2, l2g, l2b, y_ref):
    subs = [slice(r, r + POST_SUB) for r in range(0, x_ref.shape[0], POST_SUB)]
    groups = range(len(POOL_WINDOWS))
    h = [_layer_norm(x_ref[rows, :], lng[...], lnb[...]) for rows in subs]
    gates = [_dot(hn.astype(BF16), wg[...]) + bg[:, UQKV_W:] for hn in h]
    pool_out = []
    for rows in subs:
        pooled = pooled_ref[rows, :]
        pool_out.append(jnp.concatenate(
            [_dot(pooled[:, g * POOL_GC:(g + 1) * POOL_GC], wpg[g]) for g in groups],
            axis=-1) * pscale[...])
    branch_pool = [_dot(po.astype(BF16), wbp[...]) for po in pool_out]
    branch_attn = [_dot(attn_ref[rows, :], wba[...]) for rows in subs]
    mix = [jax.nn.sigmoid(g[:, :D_MODEL]) * bp + jax.nn.sigmoid(g[:, D_MODEL:]) * ba
           for g, bp, ba in zip(gates, branch_pool, branch_attn)]
    res = [_dot(m.astype(BF16), wo[...]) + bo[...] for m in mix]
    h1 = [_layer_norm(ALPHA * hn + r, l1g[...], l1b[...]) for hn, r in zip(h, res)]
    hid = [jnp.square(jnp.maximum(_dot(hn.astype(BF16), w1[...]) + b1[...], 0.0)) for hn in h1]
    f = [_dot(hd.astype(BF16), w2[...]) + b2[...] for hd in hid]
    for rows, hn, fn in zip(subs, h1, f):
        y_ref[rows, :] = _layer_norm(ALPHA * hn + fn, l2g[...], l2b[...])


def _post(first, second, weights, tm):
    tm1, tm2 = tm
    n1, n2 = first[0].shape[0] // tm1, second[0].shape[0] // tm2
    assert first[0].shape[0] == n1 * tm1 and second[0].shape[0] == n2 * tm2
    assert tm1 % POST_SUB == 0 and tm2 % POST_SUB == 0
    spec1 = lambda w: pl.BlockSpec((tm1, w), lambda i: (jnp.minimum(i, n1 - 1), 0))
    spec2 = lambda w: pl.BlockSpec((tm2, w), lambda i: (jnp.maximum(i - n1, 0), 0))
    widths = (D_MODEL, POOL_W, ATTN_W)
    return pl.pallas_call(
        functools.partial(_post_kernel, first_tiles=n1),
        grid=(n1 + n2,),
        in_specs=[spec1(w) for w in widths] + [spec2(w) for w in widths]
                 + [_whole() for _ in weights],
        out_specs=(spec1(D_MODEL), spec2(D_MODEL)),
        out_shape=(jax.ShapeDtypeStruct((n1 * tm1, D_MODEL), F32),
                   jax.ShapeDtypeStruct((n2 * tm2, D_MODEL), F32)),
        name="post",
        compiler_params=pltpu.CompilerParams(
            dimension_semantics=("arbitrary",), vmem_limit_bytes=VMEM_LIMIT_BYTES),
    )(*first, *second, *weights)


def _row(v):
    return v.reshape(1, -1).astype(F32)


def _last_rows(halo, u):
    if u.shape[1] >= POOL_STATE:
        return u[None, :, -POOL_STATE:]
    return jnp.concatenate([halo, u], axis=1)[None, :, -POOL_STATE:]


def kernel(x_prompt, x_sample, cache_k, cache_v, state_pool, meta, ln_in_g, ln_in_b, w_in, b_in,
           w_pool_grp, pool_scale, w_br_pool, w_br_attn, w_out, b_out, ln1_g, ln1_b,
           w_mlp1, b_mlp1, w_mlp2, b_mlp2, ln2_g, ln2_b):
    assert w_in.shape[0] == DEPTH
    bp, seq, _ = x_prompt.shape
    bs, dec_seq, _ = x_sample.shape
    past = cache_k.shape[2]

    lng, lnb = _row(ln_in_g), _row(ln_in_b)
    w_uqkv = w_in[0, :, :UQKV_W].astype(BF16)
    b_in_row = _row(b_in[0])
    post_weights = (
        lng, lnb, w_in[0, :, UQKV_W:].astype(BF16), b_in_row, w_pool_grp[0].astype(BF16),
        _row(pool_scale[0]),
        w_br_pool[0].astype(BF16), w_br_attn[0].astype(BF16), w_out[0].astype(BF16), _row(b_out[0]),
        _row(ln1_g[0]), _row(ln1_b[0]), w_mlp1[0].astype(BF16), _row(b_mlp1[0]),
        w_mlp2[0].astype(BF16), _row(b_mlp2[0]), _row(ln2_g[0]), _row(ln2_b[0]))

    inproj = functools.partial(_inproj, ln_g=lng, ln_b=lnb, w_uqkv=w_uqkv, b_in_row=b_in_row)
    xp = x_prompt.reshape(bp * seq, D_MODEL)
    xs = x_sample.reshape(bs * dec_seq, D_MODEL)
    u_m, _, _, kf_m, vf_m, kb_m, vb_m = inproj(
        meta.astype(F32), jnp.zeros((1, POOL_HALO, POOL_W), F32), tm=N_META)
    meta_t = lambda rows: jnp.pad(rows.T, ((0, 0), (0, KEY_BLOCK - N_META)))
    pooled_p, q_p, u_tail_p, kt_p, vt_p, ktb_p, vtb_p = _inproj_t(
        xp, lng, lnb, w_uqkv, b_in_row,
        w_uqkv[:, UQ_W:].T, b_in[0, UQ_W:UQKV_W].reshape(-1, 1).astype(F32),
        u_m, meta_t(kf_m), meta_t(vf_m), streams=bp, tm=512)
    halo_s = jnp.pad(state_pool[0], ((0, 0), (POOL_HALO - POOL_STATE, 0), (0, 0)))
    u_s, pooled_s, q_s, kf_s, vf_s, kb_s, vb_s = inproj(xs, halo_s, tm=512)

    attn_p = _attention(
        q_p.reshape(bp, seq, ATTN_W), ktb_p, vtb_p, kb_m[0].T[None], vb_m[0].T[None],
        new_keys_t=True, hist_valid=N_META, hist_dma=False, tq=128)

    u_s3 = u_s.reshape(bs, dec_seq, POOL_W)
    keys_on_lanes = lambda c: c.transpose(0, 2, 3, 1).reshape(bs, ATTN_W, past)
    attn_s = _attention(
        q_s.reshape(bs, dec_seq, ATTN_W), kb_s, vb_s,
        keys_on_lanes(cache_k[0]), keys_on_lanes(cache_v[0]),
        new_keys_t=False, hist_valid=past, hist_dma=True, tq=dec_seq)

    y_p, y_s = _post(
        (xp, pooled_p, attn_p.reshape(bp * seq, ATTN_W)),
        (xs, pooled_s, attn_s.reshape(bs * dec_seq, ATTN_W)),
        post_weights, tm=(512, 256))

    def seq_major(kv_t):
        full = kv_t.reshape(DEPTH, bp, SB_HEADS, SB_HEAD_DIM, N_META + seq)
        return full.transpose(0, 1, 4, 2, 3)

    assert seq >= POOL_STATE
    heads_s = (DEPTH, bs, dec_seq, SB_HEADS, SB_HEAD_DIM)
    return (
        y_p.reshape(bp, seq, D_MODEL),
        y_s.reshape(bs, dec_seq, D_MODEL),
        seq_major(kt_p),
        seq_major(vt_p),
        u_tail_p[None, :, POOL_HALO - POOL_STATE:],
        kf_s.reshape(heads_s),
        vf_s.reshape(heads_s),
        _last_rows(halo_s, u_s3),
    )
```

```python
import functools

import jax
import jax.numpy as jnp
from jax import lax
from jax.experimental import pallas as pl
from jax.experimental.pallas import tpu as pltpu

D_MODEL = 1024
N_META = 16
POOL_W = D_MODEL // 2
POOL_WINDOWS = (2, 4, 8, 16)
POOL_GC = POOL_W // len(POOL_WINDOWS)
POOL_STATE = max(POOL_WINDOWS) - 1
POOL_HALO = POOL_STATE + 1
SB_HEADS = 8
SB_HEAD_DIM = 64
ATTN_W = SB_HEADS * SB_HEAD_DIM
D_FF = 4 * D_MODEL
UQ_W = POOL_W + ATTN_W
UQKV_W = POOL_W + 3 * ATTN_W
LN_EPS = 1e-5
DEPTH = 1
ALPHA = (2.0 * DEPTH) ** 0.25
SB_SCALE = SB_HEAD_DIM ** -0.5
LOG2_E = 1.4426950408889634
Q_SCALE = SB_SCALE * LOG2_E

LANES = 128
KEY_BLOCK = LANES
HEAD_PAIRS = ATTN_W // LANES
STAGE_SKEW = 2
SKEW_MIN_ROWS = 256
HIST_LEAD = 2
HIST_SLOTS = 2 * HIST_LEAD + 1
ATTN_Q_PER_STEP = 4
LEAD_EXTRA = 2
POST_SUB = 256
INPROJ_SUB = 256
VMEM_LIMIT_BYTES = 56 * 1024 * 1024
DEAD_LOG2_WEIGHT = -160.0

F32 = jnp.float32
BF16 = jnp.bfloat16


def _layer_norm(x, g, b):
    mu = jnp.mean(x, axis=-1, keepdims=True)
    xc = x - mu
    var = jnp.mean(xc * xc, axis=-1, keepdims=True)
    return xc * lax.rsqrt(var + LN_EPS) * g + b


def _dot(a, b):
    return jnp.dot(a, b, preferred_element_type=F32)


def _whole():
    return pl.BlockSpec(memory_space=pltpu.VMEM)


def _pooled(halo, u):
    full = jnp.concatenate([halo, u], axis=0)
    outs = []
    for g, w in enumerate(POOL_WINDOWS):
        x = full[:, g * POOL_GC:(g + 1) * POOL_GC]
        s = x
        span = 1
        while span < w:
            s = s + pltpu.roll(s, span, axis=0)
            span *= 2
        outs.append(s[POOL_HALO:] * (1.0 / w) - x[POOL_HALO:])
    return jnp.concatenate(outs, axis=-1)


def _inproj_kernel(x_ref, halo_ref, g_ref, b_ref, w_ref, bias_ref,
                   u_ref, pooled_ref, q_ref, kf_ref, vf_ref, kb_ref, vb_ref):
    h = _layer_norm(x_ref[...], g_ref[...], b_ref[...])
    z = _dot(h.astype(BF16), w_ref[...]) + bias_ref[:, :UQKV_W]
    u = z[:, :POOL_W]
    u_ref[...] = u
    q_ref[...] = (z[:, POOL_W:POOL_W + ATTN_W] * Q_SCALE).astype(BF16)
    k = z[:, POOL_W + ATTN_W:POOL_W + 2 * ATTN_W]
    v = z[:, POOL_W + 2 * ATTN_W:]
    kf_ref[...] = k
    vf_ref[...] = v
    t = x_ref.shape[0] // halo_ref.shape[0]
    pad = jnp.zeros((kb_ref.shape[1] - t, ATTN_W), BF16)
    for s in range(halo_ref.shape[0]):
        rows = slice(s * t, (s + 1) * t)
        pooled_ref[rows, :] = _pooled(halo_ref[s], u[rows]).astype(BF16)
        kb_ref[s] = jnp.concatenate([k[rows].astype(BF16), pad], axis=0)
        vb_ref[s] = jnp.concatenate([v[rows].astype(BF16), pad], axis=0)


def _inproj(x, halo, ln_g, ln_b, w_uqkv, b_in_row, tm):
    rows = x.shape[0]
    streams = halo.shape[0]
    t = rows // streams
    tn = -(-t // KEY_BLOCK) * KEY_BLOCK
    assert rows % tm == 0 and tm % t == 0 and rows == t * streams
    row_spec = lambda w: pl.BlockSpec((tm, w), lambda i: (i, 0))
    stream_spec = lambda r, w: pl.BlockSpec((tm // t, r, w), lambda i: (i, 0, 0))
    out_shape = (
        jax.ShapeDtypeStruct((rows, POOL_W), F32),
        jax.ShapeDtypeStruct((rows, POOL_W), BF16),
        jax.ShapeDtypeStruct((rows, ATTN_W), BF16),
        jax.ShapeDtypeStruct((rows, ATTN_W), F32),
        jax.ShapeDtypeStruct((rows, ATTN_W), F32),
        jax.ShapeDtypeStruct((streams, tn, ATTN_W), BF16),
        jax.ShapeDtypeStruct((streams, tn, ATTN_W), BF16),
    )
    return pl.pallas_call(
        _inproj_kernel,
        grid=(rows // tm,),
        in_specs=[row_spec(D_MODEL), stream_spec(POOL_HALO, POOL_W),
                  _whole(), _whole(), _whole(), _whole()],
        out_specs=tuple(row_spec(s.shape[1]) for s in out_shape[:5])
                  + (stream_spec(tn, ATTN_W), stream_spec(tn, ATTN_W)),
        out_shape=out_shape,
        name="inproj",
        compiler_params=pltpu.CompilerParams(
            dimension_semantics=("arbitrary",), vmem_limit_bytes=VMEM_LIMIT_BYTES),
    )(x, halo, ln_g, ln_b, w_uqkv, b_in_row)


def _inproj_t_kernel(x_ref, g_ref, b_ref, w_uq_ref, b_uq_ref, w_kvt_ref, b_kvt_ref,
                     u_meta_ref, kt_meta_ref, vt_meta_ref,
                     pooled_ref, q_ref, u_tail_ref, kt_ref, vt_ref, ktb_ref, vtb_ref,
                     u_halo, kt_carry, vt_carry):
    j = pl.program_id(1)
    tm = x_ref.shape[0]
    t = kt_ref.shape[2] - N_META
    lane = lax.broadcasted_iota(jnp.int32, (ATTN_W, LANES), 1)

    @pl.when(j == 0)
    def _():
        u_halo[...] = u_meta_ref[...]
        kt_carry[...] = kt_meta_ref[...]
        vt_carry[...] = vt_meta_ref[...]

    def shifted(carry, cur):
        rolled = pltpu.roll(cur, N_META, axis=1)
        out = jnp.concatenate(
            [jnp.where(lane < N_META, carry, rolled[:, :LANES]), rolled[:, LANES:]], axis=1)
        return out, rolled[:, :LANES]

    subs = [slice(r, r + INPROJ_SUB) for r in range(0, tm, INPROJ_SUB)]
    hb = [_layer_norm(x_ref[rows, :], g_ref[...], b_ref[...]).astype(BF16) for rows in subs]
    z, zt = [], []
    for h in hb:
        z.append(_dot(h, w_uq_ref[:, :UQ_W]))
        zt.append(lax.dot_general(w_kvt_ref[...], h, (((1,), (1,)), ((), ())),
                                  preferred_element_type=F32))
    halo, kt_c, vt_c = u_halo[...], kt_carry[...], vt_carry[...]
    for n, rows in enumerate(subs):
        zn = z[n] + b_uq_ref[:, :UQ_W]
        u = zn[:, :POOL_W]
        q_ref[rows, :] = (zn[:, POOL_W:] * Q_SCALE).astype(BF16)
        pooled_ref[rows, :] = _pooled(halo, u).astype(BF16)
        halo = u[INPROJ_SUB - POOL_HALO:]
        ztn = zt[n] + b_kvt_ref[...]
        kt, vt = ztn[:ATTN_W], ztn[ATTN_W:]
        for c in range(INPROJ_SUB // KEY_BLOCK):
            cols = slice(c * KEY_BLOCK, (c + 1) * KEY_BLOCK)
            blk = rows.start // KEY_BLOCK + c
            ktb_ref[0, blk] = kt[:, cols].astype(BF16)
            vtb_ref[0, blk] = vt[:, cols].astype(BF16)
        out_cols = pl.ds(pl.multiple_of(j * tm + rows.start, LANES), INPROJ_SUB)
        kt_ref[0, :, out_cols], kt_c = shifted(kt_c, kt)
        vt_ref[0, :, out_cols], vt_c = shifted(vt_c, vt)
    u_halo[...] = halo
    u_tail_ref[0] = halo
    kt_carry[...] = kt_c
    vt_carry[...] = vt_c

    @pl.when(j == pl.num_programs(1) - 1)
    def _():
        kt_ref[0, :, t:] = kt_c[:, :N_META]
        vt_ref[0, :, t:] = vt_c[:, :N_META]


def _inproj_t(x, ln_g, ln_b, w_uq, b_uq, w_kvt, b_kvt, u_meta, kt_meta, vt_meta, streams, tm):
    rows = x.shape[0]
    t = rows // streams
    assert rows == streams * t and t % tm == 0 and POOL_HALO == N_META
    assert tm % INPROJ_SUB == 0 and INPROJ_SUB % KEY_BLOCK == 0
    per = t // tm
    kb = tm // KEY_BLOCK
    row_spec = lambda w: pl.BlockSpec((tm, w), lambda s, j: (s * per + j, 0))
    t_spec = pl.BlockSpec((1, ATTN_W, N_META + t), lambda s, j: (s, 0, 0))
    tb_spec = pl.BlockSpec((1, kb, ATTN_W, KEY_BLOCK), lambda s, j: (s, j, 0, 0))
    out_shape = (
        jax.ShapeDtypeStruct((rows, POOL_W), BF16),
        jax.ShapeDtypeStruct((rows, ATTN_W), BF16),
        jax.ShapeDtypeStruct((streams, POOL_HALO, POOL_W), F32),
        jax.ShapeDtypeStruct((streams, ATTN_W, N_META + t), F32),
        jax.ShapeDtypeStruct((streams, ATTN_W, N_META + t), F32),
        jax.ShapeDtypeStruct((streams, t // KEY_BLOCK, ATTN_W, KEY_BLOCK), BF16),
        jax.ShapeDtypeStruct((streams, t // KEY_BLOCK, ATTN_W, KEY_BLOCK), BF16),
    )
    return pl.pallas_call(
        _inproj_t_kernel,
        grid=(streams, per),
        in_specs=[row_spec(D_MODEL)] + [_whole()] * 9,
        out_specs=(row_spec(POOL_W), row_spec(ATTN_W),
                   pl.BlockSpec((1, POOL_HALO, POOL_W), lambda s, j: (s, 0, 0)),
                   t_spec, t_spec, tb_spec, tb_spec),
        out_shape=out_shape,
        scratch_shapes=[pltpu.VMEM((POOL_HALO, POOL_W), F32),
                        pltpu.VMEM((ATTN_W, LANES), F32), pltpu.VMEM((ATTN_W, LANES), F32)],
        name="inproj_t",
        compiler_params=pltpu.CompilerParams(
            dimension_semantics=("arbitrary", "arbitrary"), vmem_limit_bytes=VMEM_LIMIT_BYTES),
    )(x, ln_g, ln_b, w_uq, b_uq, w_kvt, b_kvt, u_meta, kt_meta, vt_meta)


def _attn_kernel(*refs, q_per_step, **static):
    for sub in range(q_per_step):
        _attn_q_block(*refs, sub=sub, q_per_step=q_per_step, **static)


def _attn_q_block(q_ref, kn_ref, vn_ref, kh_ref, vh_ref, su_ref, o_ref, *scratch, sub, q_per_step,
                  tq, single_q_block, new_keys_t, n_hist_blocks, hist_valid, hist_dma):
    stream = pl.program_id(0)
    i = pl.program_id(1) * q_per_step + sub
    q_rows = slice(sub * tq, (sub + 1) * tq)

    n_lead_hist = min(HIST_LEAD, n_hist_blocks) if hist_dma else 0

    def hist_copies(of_stream, jj, slot):
        hbuf, hsem = scratch[2:]
        start = pl.multiple_of((n_hist_blocks - 1 - jj) * KEY_BLOCK, KEY_BLOCK)
        return [pltpu.make_async_copy(src.at[of_stream, :, pl.ds(start, KEY_BLOCK)],
                                      hbuf.at[slot, which], hsem.at[slot, which])
                for which, src in enumerate((kh_ref, vh_ref))]

    def lead_slot(of_stream, jj):
        return lax.rem(of_stream, 2) * n_lead_hist + jj

    def lead_hist_copies(of_stream):
        return [copy for jj in range(n_lead_hist)
                for copy in hist_copies(of_stream, jj, lead_slot(of_stream, jj))]

    if hist_dma:
        assert single_q_block

        @pl.when(stream == 0)
        def _():
            for copy in lead_hist_copies(stream):
                copy.start()

        for copy in lead_hist_copies(stream):
            copy.wait()

        @pl.when(stream + 1 < pl.num_programs(0))
        def _():
            for copy in lead_hist_copies(stream + 1):
                copy.start()

    lane = lax.broadcasted_iota(jnp.int32, (tq, LANES), 1)
    even = lane < SB_HEAD_DIM
    qs = []
    for p in range(HEAD_PAIRS):
        q = q_ref[0, q_rows, p * LANES:(p + 1) * LANES]
        zero = jnp.zeros_like(q)
        qs.append(jnp.concatenate([jnp.where(even, q, zero), jnp.where(even, zero, q)], axis=0))
    su = su_ref[...]

    row = lax.broadcasted_iota(jnp.int32, (2 * tq, KEY_BLOCK), 0)
    col = lax.broadcasted_iota(jnp.int32, (2 * tq, KEY_BLOCK), 1)
    q_pos = jnp.where(row >= tq, row - tq, row) + i * tq

    pairs = range(HEAD_PAIRS)

    nt_dims = (((1,), (1,)), ((), ()))

    c_ref, acc_ref = scratch[:2]

    def step(blocks, first=False):
        sl = [slice(p * LANES, (p + 1) * LANES) for p in pairs]
        chains = [(b, p) for b in range(len(blocks)) for p in pairs]
        s, tl, carry = {}, {}, {}

        def scores(b, p):
            k_blk, _, _, transposed = blocks[b]
            if transposed:
                s[b, p] = _dot(qs[p], k_blk[sl[p], :])
            else:
                s[b, p] = lax.dot_general(qs[p], k_blk[:, sl[p]], nt_dims,
                                          preferred_element_type=F32)

        def cumulate(b, p):
            vis = blocks[b][2]
            z2 = s[b, p]
            sp = jnp.maximum(z2, 0.0) + jnp.log(1.0 + jnp.exp2(-jnp.abs(z2))) * LOG2_E
            if vis is not None:
                sp = jnp.where(vis, sp, 0.0)
            hi = sp.astype(BF16)
            lo = (sp - hi.astype(F32)).astype(BF16)
            tl[b, p] = _dot(jnp.concatenate([hi, lo], axis=1), su)

        def weigh(b, p):
            _, v_blk, vis, transposed = blocks[b]
            if b > 0:
                c, acc = carry[p]
            elif first:
                c = acc = jnp.zeros((2 * tq, LANES), F32)
            else:
                c, acc = c_ref[p], acc_ref[p]
            w = jnp.exp2(s[b, p] + tl[b, p][:, :KEY_BLOCK] + c)
            if vis is not None:
                w = jnp.where(vis, w, 0.0)
            wb = w.astype(BF16)
            if transposed:
                pv = lax.dot_general(wb, v_blk[sl[p], :], nt_dims, preferred_element_type=F32)
            else:
                pv = _dot(wb, v_blk[:, sl[p]])
            carry[p] = (c + tl[b, p][:, KEY_BLOCK:], acc + pv)
            if b == len(blocks) - 1:
                c_ref[p], acc_ref[p] = carry[p]

        n = len(chains)
        skew = n if 2 * tq < SKEW_MIN_ROWS else STAGE_SKEW
        for t in range(n + 2 * skew):
            if t < n:
                scores(*chains[t])
            if 0 <= t - skew < n:
                cumulate(*chains[t - skew])
            if 0 <= t - 2 * skew < n:
                weigh(*chains[t - 2 * skew])
        c_max = functools.reduce(jnp.maximum, [carry[p][0] for p in pairs])
        return (jnp.max(c_max) > DEAD_LOG2_WEIGHT).astype(jnp.int32)

    def new_block(j, vis):
        if new_keys_t:
            return kn_ref[0, j], vn_ref[0, j], vis, True
        start = pl.multiple_of(j * KEY_BLOCK, KEY_BLOCK)
        return (kn_ref[0, pl.ds(start, KEY_BLOCK), :], vn_ref[0, pl.ds(start, KEY_BLOCK), :], vis,
                False)

    def hist_block(slot):
        hbuf = scratch[2]
        return hbuf[slot, 0].astype(BF16), hbuf[slot, 1].astype(BF16), None, True

    def new_step(js):
        return step([new_block(j, None) for j in js])

    n_diag = max(tq // KEY_BLOCK, 1)
    if single_q_block:
        first_diag = 0
    else:
        first_diag = i * (tq // KEY_BLOCK) if tq >= KEY_BLOCK else (i * tq) // KEY_BLOCK

    def lead(extra):
        blocks = [new_block(first_diag + d, (first_diag + d) * KEY_BLOCK + col < q_pos)
                  for d in reversed(range(n_diag))]
        blocks += [new_block(first_diag - 1 - e, None) for e in range(extra)]
        if single_q_block:
            blocks += [hist_block(lead_slot(stream, jj)) for jj in range(n_lead_hist)]
        return step(blocks, first=True)

    if single_q_block:
        live, n_rest = lead(0), 0
    else:
        has_extra = first_diag >= LEAD_EXTRA
        live = lax.cond(has_extra, lambda: lead(LEAD_EXTRA), lambda: lead(0))
        n_rest = first_diag - jnp.where(has_extra, LEAD_EXTRA, 0)

    def visit(n_blocks, body, live):
        def cond(state):
            return jnp.logical_and(state[0] < n_blocks, state[1] > 0)

        return lax.while_loop(cond, lambda state: (state[0] + 1, body(state[0])),
                              (jnp.int32(0), live))[1]

    if not single_q_block:
        live = visit(lax.shift_right_logical(n_rest, 1),
                     lambda jj: new_step([n_rest - 1 - 2 * jj, n_rest - 2 - 2 * jj]), live)
        live = visit(jnp.bitwise_and(n_rest, 1), lambda jj: new_step([0]), live)

    partial = hist_valid % KEY_BLOCK

    if hist_dma:
        assert not partial

        def hist_body(jj):
            copies = hist_copies(stream, n_lead_hist + jj, HIST_SLOTS - 1)
            for copy in copies:
                copy.start()
            for copy in copies:
                copy.wait()
            return step([hist_block(HIST_SLOTS - 1)])

        visit(n_hist_blocks - n_lead_hist, hist_body, live)
    else:
        assert n_hist_blocks == 1
        vis = col < partial if partial else None
        visit(1, lambda jj: step([(kh_ref[0].astype(BF16), vh_ref[0].astype(BF16), vis, True)]),
              live)

    for p in range(HEAD_PAIRS):
        acc = acc_ref[p]
        o_ref[0, q_rows, p * LANES:(p + 1) * LANES] = (
            jnp.where(even, acc[:tq], acc[tq:]).astype(o_ref.dtype))


def _cumsum_matrix():
    r = lax.broadcasted_iota(jnp.int32, (2 * KEY_BLOCK, KEY_BLOCK + LANES), 0) % KEY_BLOCK
    c = lax.broadcasted_iota(jnp.int32, (2 * KEY_BLOCK, KEY_BLOCK + LANES), 1)
    return jnp.where((c >= KEY_BLOCK) | (r >= c), -1.0, 0.0).astype(BF16)


def _attention(q, k_new, v_new, k_hist, v_hist, *, new_keys_t, hist_valid, hist_dma, tq):
    b, t, _ = q.shape
    q_per_step = ATTN_Q_PER_STEP if t % (tq * ATTN_Q_PER_STEP) == 0 else 1
    tstep = tq * q_per_step
    assert t % tstep == 0
    if new_keys_t:
        assert k_new.shape[1] * KEY_BLOCK == t
        new_spec = pl.BlockSpec((1,) + k_new.shape[1:], lambda bi, i: (bi, 0, 0, 0))
    else:
        assert k_new.shape[1] % KEY_BLOCK == 0 and k_new.shape[1] >= t
        new_spec = pl.BlockSpec((1,) + k_new.shape[1:], lambda bi, i: (bi, 0, 0))
    hl = k_hist.shape[2]
    assert hl % KEY_BLOCK == 0 and hist_valid <= hl
    scratch = [pltpu.VMEM((HEAD_PAIRS, 2 * tq, LANES), F32)] * 2
    if hist_dma:
        hist_spec = pl.BlockSpec(memory_space=pl.ANY)
        scratch += [pltpu.VMEM((HIST_SLOTS, 2, ATTN_W, KEY_BLOCK), k_hist.dtype),
                    pltpu.SemaphoreType.DMA((HIST_SLOTS, 2))]
    else:
        hist_spec = pl.BlockSpec((1, ATTN_W, KEY_BLOCK), lambda bi, i: (0, 0, 0))
    kern = functools.partial(_attn_kernel, q_per_step=q_per_step, tq=tq, single_q_block=(t == tq),
                             new_keys_t=new_keys_t, n_hist_blocks=hl // KEY_BLOCK,
                             hist_valid=hist_valid, hist_dma=hist_dma)
    return pl.pallas_call(
        kern,
        grid=(b, t // tstep),
        in_specs=[pl.BlockSpec((1, tstep, ATTN_W), lambda bi, i: (bi, i, 0)),
                  new_spec, new_spec, hist_spec, hist_spec, _whole()],
        out_specs=pl.BlockSpec((1, tstep, ATTN_W), lambda bi, i: (bi, i, 0)),
        out_shape=jax.ShapeDtypeStruct((b, t, ATTN_W), BF16),
        scratch_shapes=scratch,
        name="attn",
        compiler_params=pltpu.CompilerParams(
            dimension_semantics=("arbitrary", "arbitrary"),
            vmem_limit_bytes=VMEM_LIMIT_BYTES),
    )(q, k_new, v_new, k_hist, v_hist, _cumsum_matrix())


def _post_kernel(*refs, first_tiles):
    (x1, pooled1, attn1, x2, pooled2, attn2), weights, (y1, y2) = refs[:6], refs[6:-2], refs[-2:]
    step = pl.program_id(0)

    @pl.when(step < first_tiles)
    def _():
        _post_tile(x1, pooled1, attn1, *weights, y1)

    @pl.when(step >= first_tiles)
    def _():
        _post_tile(x2, pooled2, attn2, *weights, y2)


def _post_tile(x_ref, pooled_ref, attn_ref, lng, lnb, wg, bg, wpg, pscale, wbp, wba, wo, bo,
               l1g, l1b, w1, b1, w2, b2, l2g, l2b, y_ref):
    subs = [slice(r, r + POST_SUB) for r in range(0, x_ref.shape[0], POST_SUB)]
    groups = range(len(POOL_WINDOWS))
    h = [_layer_norm(x_ref[rows, :], lng[...], lnb[...]) for rows in subs]
    gates = [_dot(hn.astype(BF16), wg[...]) + bg[:, UQKV_W:] for hn in h]
    pool_out = []
    for rows in subs:
        pooled = pooled_ref[rows, :]
        pool_out.append(jnp.concatenate(
            [_dot(pooled[:, g * POOL_GC:(g + 1) * POOL_GC], wpg[g]) for g in groups],
            axis=-1) * pscale[...])
    branch_pool = [_dot(po.astype(BF16), wbp[...]) for po in pool_out]
    branch_attn = [_dot(attn_ref[rows, :], wba[...]) for rows in subs]
    mix = [jax.nn.sigmoid(g[:, :D_MODEL]) * bp + jax.nn.sigmoid(g[:, D_MODEL:]) * ba
           for g, bp, ba in zip(gates, branch_pool, branch_attn)]
    res = [_dot(m.astype(BF16), wo[...]) + bo[...] for m in mix]
    h1 = [_layer_norm(ALPHA * hn + r, l1g[...], l1b[...]) for hn, r in zip(h, res)]
    hid = [jnp.square(jnp.maximum(_dot(hn.astype(BF16), w1[...]) + b1[...], 0.0)) for hn in h1]
    f = [_dot(hd.astype(BF16), w2[...]) + b2[...] for hd in hid]
    for rows, hn, fn in zip(subs, h1, f):
        y_ref[rows, :] = _layer_norm(ALPHA * hn + fn, l2g[...], l2b[...])


def _post(first, second, weights, tm):
    tm1, tm2 = tm
    n1, n2 = first[0].shape[0] // tm1, second[0].shape[0] // tm2
    assert first[0].shape[0] == n1 * tm1 and second[0].shape[0] == n2 * tm2
    assert tm1 % POST_SUB == 0 and tm2 % POST_SUB == 0
    spec1 = lambda w: pl.BlockSpec((tm1, w), lambda i: (jnp.minimum(i, n1 - 1), 0))
    spec2 = lambda w: pl.BlockSpec((tm2, w), lambda i: (jnp.maximum(i - n1, 0), 0))
    widths = (D_MODEL, POOL_W, ATTN_W)
    return pl.pallas_call(
        functools.partial(_post_kernel, first_tiles=n1),
        grid=(n1 + n2,),
        in_specs=[spec1(w) for w in widths] + [spec2(w) for w in widths]
                 + [_whole() for _ in weights],
        out_specs=(spec1(D_MODEL), spec2(D_MODEL)),
        out_shape=(jax.ShapeDtypeStruct((n1 * tm1, D_MODEL), F32),
                   jax.ShapeDtypeStruct((n2 * tm2, D_MODEL), F32)),
        name="post",
        compiler_params=pltpu.CompilerParams(
            dimension_semantics=("arbitrary",), vmem_limit_bytes=VMEM_LIMIT_BYTES),
    )(*first, *second, *weights)


def _row(v):
    return v.reshape(1, -1).astype(F32)


def _last_rows(halo, u):
    if u.shape[1] >= POOL_STATE:
        return u[None, :, -POOL_STATE:]
    return jnp.concatenate([halo, u], axis=1)[None, :, -POOL_STATE:]


def kernel(x_prompt, x_sample, cache_k, cache_v, state_pool, meta, ln_in_g, ln_in_b, w_in, b_in,
           w_pool_grp, pool_scale, w_br_pool, w_br_attn, w_out, b_out, ln1_g, ln1_b,
           w_mlp1, b_mlp1, w_mlp2, b_mlp2, ln2_g, ln2_b):
    assert w_in.shape[0] == DEPTH
    bp, seq, _ = x_prompt.shape
    bs, dec_seq, _ = x_sample.shape
    past = cache_k.shape[2]

    lng, lnb = _row(ln_in_g), _row(ln_in_b)
    w_uqkv = w_in[0, :, :UQKV_W].astype(BF16)
    b_in_row = _row(b_in[0])
    post_weights = (
        lng, lnb, w_in[0, :, UQKV_W:].astype(BF16), b_in_row, w_pool_grp[0].astype(BF16),
        _row(pool_scale[0]),
        w_br_pool[0].astype(BF16), w_br_attn[0].astype(BF16), w_out[0].astype(BF16), _row(b_out[0]),
        _row(ln1_g[0]), _row(ln1_b[0]), w_mlp1[0].astype(BF16), _row(b_mlp1[0]),
        w_mlp2[0].astype(BF16), _row(b_mlp2[0]), _row(ln2_g[0]), _row(ln2_b[0]))

    inproj = functools.partial(_inproj, ln_g=lng, ln_b=lnb, w_uqkv=w_uqkv, b_in_row=b_in_row)
    xp = x_prompt.reshape(bp * seq, D_MODEL)
    xs = x_sample.reshape(bs * dec_seq, D_MODEL)
    u_m, _, _, kf_m, vf_m, kb_m, vb_m = inproj(
        meta.astype(F32), jnp.zeros((1, POOL_HALO, POOL_W), F32), tm=N_META)
    meta_t = lambda rows: jnp.pad(rows.T, ((0, 0), (0, KEY_BLOCK - N_META)))
    pooled_p, q_p, u_tail_p, kt_p, vt_p, ktb_p, vtb_p = _inproj_t(
        xp, lng, lnb, w_uqkv, b_in_row,
        w_uqkv[:, UQ_W:].T, b_in[0, UQ_W:UQKV_W].reshape(-1, 1).astype(F32),
        u_m, meta_t(kf_m), meta_t(vf_m), streams=bp, tm=512)
    halo_s = jnp.pad(state_pool[0], ((0, 0), (POOL_HALO - POOL_STATE, 0), (0, 0)))
    u_s, pooled_s, q_s, kf_s, vf_s, kb_s, vb_s = inproj(xs, halo_s, tm=512)

    attn_p = _attention(
        q_p.reshape(bp, seq, ATTN_W), ktb_p, vtb_p, kb_m[0].T[None], vb_m[0].T[None],
        new_keys_t=True, hist_valid=N_META, hist_dma=False, tq=128)

    u_s3 = u_s.reshape(bs, dec_seq, POOL_W)
    keys_on_lanes = lambda c: c.transpose(0, 2, 3, 1).reshape(bs, ATTN_W, past)
    attn_s = _attention(
        q_s.reshape(bs, dec_seq, ATTN_W), kb_s, vb_s,
        keys_on_lanes(cache_k[0]), keys_on_lanes(cache_v[0]),
        new_keys_t=False, hist_valid=past, hist_dma=True, tq=dec_seq)

    y_p, y_s = _post(
        (xp, pooled_p, attn_p.reshape(bp * seq, ATTN_W)),
        (xs, pooled_s, attn_s.reshape(bs * dec_seq, ATTN_W)),
        post_weights, tm=(512, 256))

    def seq_major(kv_t):
        full = kv_t.reshape(DEPTH, bp, SB_HEADS, SB_HEAD_DIM, N_META + seq)
        return full.transpose(0, 1, 4, 2, 3)

    assert seq >= POOL_STATE
    heads_s = (DEPTH, bs, dec_seq, SB_HEADS, SB_HEAD_DIM)
    return (
        y_p.reshape(bp, seq, D_MODEL),
        y_s.reshape(bs, dec_seq, D_MODEL),
        seq_major(kt_p),
        seq_major(vt_p),
        u_tail_p[None, :, POOL_HALO - POOL_STATE:],
        kf_s.reshape(heads_s),
        vf_s.reshape(heads_s),
        _last_rows(halo_s, u_s3),
    )
```
